```python
import jax, jax.numpy as jnp
from jax import lax
import numpy as np

D_MODEL = 4096
BATCH = 4
SEQ = 2048
DEPTH = 2
DEC_BATCH = 8
DEC_SEQ = 8
PAST_LEN = 16384
PAGE_SIZE = 128

N_A_LAYERS = DEPTH // 2
N_B_LAYERS = DEPTH - N_A_LAYERS
HEAD_DIM = 128
N_HEADS = D_MODEL // HEAD_DIM
KV_GROUPS = 4
HEADS_PER_GROUP = N_HEADS // KV_GROUPS
N_BRANCH = 3
N_KV_PROJ = 2 * N_BRANCH
CMP_BLOCK = 32
CMP_STRIDE = 16
CMP_HID = HEAD_DIM
SEL_BLOCK = 64
N_SELECT = 16
N_INIT_BLOCKS = 1
N_LOCAL_BLOCKS = 2
WINDOW = 512
CONV_K = 31
FFN_K = 3
D_FF = 11008
QBLK = 64
WBLK = 128
EPS = 1e-6
NEG = -1e30
FORCE = 1e9
SCALE = HEAD_DIM ** -0.5
F32 = jnp.float32

kernel_name = 'yoco_conformer_nsa_convffn_step'


def rms_norm(x, g):
    xf = x.astype(F32)
    y = xf * lax.rsqrt(jnp.mean(xf * xf, axis=-1, keepdims=True) + EPS)
    return (y * g.astype(F32)).astype(x.dtype)


def layer_norm(x, g, b):
    xf = x.astype(F32)
    mu = jnp.mean(xf, axis=-1, keepdims=True)
    xc = xf - mu
    y = xc * lax.rsqrt(jnp.mean(xc * xc, axis=-1, keepdims=True) + EPS)
    return (y * g.astype(F32) + b.astype(F32)).astype(x.dtype)


def alibi_slopes():
    m = jnp.exp2(-8.0 * jnp.arange(1, N_HEADS + 1, dtype=F32) / N_HEADS)
    return m.reshape(KV_GROUPS, HEADS_PER_GROUP)


def causal_dwconv(x, past, w):
    xx = jnp.concatenate([past.astype(x.dtype), x], axis=1)
    y = lax.conv_general_dilated(xx, w[:, None, :].astype(x.dtype), window_strides=(1,), padding='VALID',
                                 dimension_numbers=('NWC', 'WIO', 'NWC'), feature_group_count=x.shape[-1])
    return y, xx[:, -(w.shape[0] - 1):]


def conformer_conv(h, past, w_pw1, w_dw, ln_g, ln_b, w_pw2):
    a, b = jnp.split(h @ w_pw1, 2, axis=-1)
    u = a * jax.nn.sigmoid(b)
    c, new_past = causal_dwconv(u, past, w_dw)
    return jax.nn.silu(layer_norm(c, ln_g, ln_b)) @ w_pw2, new_past


def conv_ffn(h, past, w_up, w_dw, w_down):
    c, new_past = causal_dwconv(h @ w_up, past, w_dw)
    a, b = jnp.split(c, 2, axis=-1)
    return (jax.nn.silu(a) * b) @ w_down, new_past


def shared_kv(x, g_kv, w_kv):
    B, T = x.shape[:2]
    kv = (rms_norm(x, g_kv) @ w_kv).reshape(B, T, N_KV_PROJ, KV_GROUPS, HEAD_DIM)
    return kv[:, :, 0], kv[:, :, 1], kv[:, :, 2], kv[:, :, 3], kv[:, :, 4], kv[:, :, 5]


def compress(x, pe, w1, w2):
    B, T = x.shape[:2]
    n_half = -(-T // CMP_STRIDE)
    x = jnp.pad(x, ((0, 0), (0, n_half * CMP_STRIDE - T), (0, 0), (0, 0)))
    halves = x.reshape(B, n_half, CMP_STRIDE, KV_GROUPS, HEAD_DIM)
    pe2 = pe.reshape(2, CMP_STRIDE, 1, HEAD_DIM)
    w12 = w1.reshape(2, CMP_STRIDE, HEAD_DIM, CMP_HID)
    lo = jnp.einsum('bnjgd,jde->bnge', halves + pe2[0], w12[0])
    hi = jnp.einsum('bnjgd,jde->bnge', halves + pe2[1], w12[1])
    return jnp.einsum('bnge,ed->bngd', jax.nn.silu(lo[:, :-1] + hi[:, 1:]), w2)


def cmp_attention(q, pos, kc, vc, slopes):
    B, T = q.shape[:2]
    qg = q.reshape(B, T, KV_GROUPS, HEADS_PER_GROUP, HEAD_DIM)
    s = jnp.einsum('btghd,bngd->btghn', qg, kc).astype(F32) * SCALE
    end = jnp.arange(kc.shape[1], dtype=jnp.int32) * CMP_STRIDE + (CMP_BLOCK - 1)
    dist = (pos[:, None] - end[None, :])[None, :, None, None, :]
    valid = dist >= 0
    s = jnp.where(valid, s - slopes[None, None, :, :, None] * dist.astype(F32), NEG)
    p = jax.nn.softmax(s, axis=-1) * valid
    o = jnp.einsum('btghn,bngd->btghd', p.astype(vc.dtype), vc)
    return o.reshape(B, T, N_HEADS, HEAD_DIM), p


def select_blocks(p_cmp, pos, n_blocks):
    nc = p_cmp.shape[-1]
    ci = jnp.arange(nc, dtype=jnp.int32)[:, None]
    sj = jnp.arange(n_blocks, dtype=jnp.int32)[None, :]
    lo = jnp.maximum(ci * CMP_STRIDE, sj * SEL_BLOCK)
    hi = jnp.minimum(ci * CMP_STRIDE + CMP_BLOCK, (sj + 1) * SEL_BLOCK)
    cover = jnp.maximum(hi - lo, 0).astype(F32) / CMP_STRIDE
    score = jnp.einsum('btghn,nj->btgj', p_cmp, cover)
    cur = (pos // SEL_BLOCK)[:, None]
    valid = sj * SEL_BLOCK <= pos[:, None]
    forced = (sj < N_INIT_BLOCKS) | ((cur - sj >= 0) & (cur - sj < N_LOCAL_BLOCKS))
    score = jnp.where(forced[None, :, None, :], FORCE, score)
    score = jnp.where(valid[None, :, None, :], score, NEG)
    _, idx = lax.top_k(score, min(N_SELECT, n_blocks))
    return idx


def slc_attend(q, pos, idx, kb, vb, slopes):
    B, Q = q.shape[:2]
    n = idx.shape[-1]
    qg = q.reshape(B, Q, KV_GROUPS, HEADS_PER_GROUP, HEAD_DIM)
    s = jnp.einsum('bqghd,bqgnld->bqghnl', qg, kb).astype(F32) * SCALE
    key_pos = idx[..., None] * SEL_BLOCK + jnp.arange(SEL_BLOCK, dtype=jnp.int32)
    dist = (pos[None, :, None, None, None] - key_pos)[:, :, :, None]
    valid = dist >= 0
    s = jnp.where(valid, s - slopes[None, None, :, :, None, None] * dist.astype(F32), NEG)
    p = jax.nn.softmax(s.reshape(B, Q, KV_GROUPS, HEADS_PER_GROUP, n * SEL_BLOCK), axis=-1)
    p = p.reshape(s.shape).astype(vb.dtype)
    o = jnp.einsum('bqghnl,bqgnld->bqghd', p, vb)
    return o.reshape(B, Q, N_HEADS, HEAD_DIM)


def band_attend(q, pos_q, k, v, pos_k, slopes):
    B, Q = q.shape[:2]
    qg = q.reshape(B, Q, KV_GROUPS, HEADS_PER_GROUP, HEAD_DIM)
    s = jnp.einsum('bqghd,bkgd->bqghk', qg, k).astype(F32) * SCALE
    dist = pos_q[:, None] - pos_k[None, :]
    valid = ((dist >= 0) & (dist < WINDOW) & (pos_k[None, :] >= 0))[None, :, None, None, :]
    dist = dist.astype(F32)[None, :, None, None, :]
    s = jnp.where(valid, s - slopes[None, None, :, :, None] * dist, NEG)
    p = jax.nn.softmax(s, axis=-1).astype(v.dtype)
    return jnp.einsum('bqghk,bkgd->bqghd', p, v).reshape(B, Q, N_HEADS, HEAD_DIM)


def make_prompt_side(cmp_k, cmp_v, slc_k, slc_v, win_k, win_v, cmp_pe_k, cmp_w1_k, cmp_w2_k,
                     cmp_pe_v, cmp_w1_v, cmp_w2_v, slopes):
    B, T = slc_k.shape[:2]
    kc = compress(cmp_k, cmp_pe_k, cmp_w1_k, cmp_w2_k)
    vc = compress(cmp_v, cmp_pe_v, cmp_w1_v, cmp_w2_v)
    n_blocks = T // SEL_BLOCK
    kblk = slc_k.reshape(B, n_blocks, SEL_BLOCK, KV_GROUPS, HEAD_DIM)
    vblk = slc_v.reshape(B, n_blocks, SEL_BLOCK, KV_GROUPS, HEAD_DIM)
    bi = jnp.arange(B)[:, None, None, None]
    gi = jnp.arange(KV_GROUPS)[None, None, :, None]

    def slc_fn(q, pos, idx):
        nq = T // QBLK
        n = idx.shape[-1]

        def body(args):
            qc, pc, ic = args
            return slc_attend(qc, pc, ic, kblk[bi, ic, :, gi], vblk[bi, ic, :, gi], slopes)

        o = lax.map(body, (q.reshape(B, nq, QBLK, N_HEADS, HEAD_DIM).transpose(1, 0, 2, 3, 4),
                           pos.reshape(nq, QBLK),
                           idx.reshape(B, nq, QBLK, KV_GROUPS, n).transpose(1, 0, 2, 3, 4)))
        return o.transpose(1, 0, 2, 3, 4).reshape(B, T, N_HEADS, HEAD_DIM)

    span = WBLK + WINDOW
    pad = ((0, 0), (WINDOW, 0), (0, 0), (0, 0))
    kp = jnp.pad(win_k, pad)
    vp = jnp.pad(win_v, pad)

    def win_fn(q, pos):
        nb = T // WBLK

        def body(args):
            qc, pc = args
            start = pc[0]
            kb = lax.dynamic_slice_in_dim(kp, start, span, axis=1)
            vb = lax.dynamic_slice_in_dim(vp, start, span, axis=1)
            pk = start - WINDOW + jnp.arange(span, dtype=jnp.int32)
            return band_attend(qc, pc, kb, vb, pk, slopes)

        o = lax.map(body, (q.reshape(B, nb, WBLK, N_HEADS, HEAD_DIM).transpose(1, 0, 2, 3, 4),
                           pos.reshape(nb, WBLK)))
        return o.transpose(1, 0, 2, 3, 4).reshape(B, T, N_HEADS, HEAD_DIM)

    keep = min(WINDOW, T)
    return (kc, vc, n_blocks, slc_fn, win_fn), win_k[:, -keep:], win_v[:, -keep:]


def make_sample_side(cmp_k, cmp_v, slc_k, slc_v, win_k, win_v, cache_cmp_k, cache_cmp_v,
                     cache_slc_k, cache_slc_v, cache_win_k, cache_win_v, page_table,
                     cmp_pe_k, cmp_w1_k, cmp_w2_k, cmp_pe_v, cmp_w1_v, cmp_w2_v, slopes):
    B, S = slc_k.shape[:2]
    n_pages = page_table.shape[1]
    past = n_pages * PAGE_SIZE

    def past_rows(pool, like):
        return pool[page_table].reshape(B, past, KV_GROUPS, HEAD_DIM).astype(like.dtype)

    kc = compress(jnp.concatenate([past_rows(cache_cmp_k, cmp_k), cmp_k], axis=1), cmp_pe_k, cmp_w1_k, cmp_w2_k)
    vc = compress(jnp.concatenate([past_rows(cache_cmp_v, cmp_v), cmp_v], axis=1), cmp_pe_v, cmp_w1_v, cmp_w2_v)

    bpp = PAGE_SIZE // SEL_BLOCK
    n_past_blocks = past // SEL_BLOCK
    n_new_blocks = -(-S // SEL_BLOCK)
    padn = ((0, 0), (0, n_new_blocks * SEL_BLOCK - S), (0, 0), (0, 0))
    new_kblk = jnp.pad(slc_k, padn).reshape(B, n_new_blocks, SEL_BLOCK, KV_GROUPS, HEAD_DIM)
    new_vblk = jnp.pad(slc_v, padn).reshape(B, n_new_blocks, SEL_BLOCK, KV_GROUPS, HEAD_DIM)
    pool_kblk = cache_slc_k.reshape(-1, SEL_BLOCK, KV_GROUPS, HEAD_DIM)
    pool_vblk = cache_slc_v.reshape(-1, SEL_BLOCK, KV_GROUPS, HEAD_DIM)
    bi = jnp.arange(B)[:, None, None, None]
    gi = jnp.arange(KV_GROUPS)[None, None, :, None]

    def slc_fn(q, pos, idx):
        in_past = (idx < n_past_blocks)[..., None, None]
        pidx = jnp.minimum(idx, n_past_blocks - 1)
        phys = page_table[bi, pidx // bpp] * bpp + pidx % bpp
        nidx = jnp.clip(idx - n_past_blocks, 0, n_new_blocks - 1)
        kb = jnp.where(in_past, pool_kblk[phys, :, gi].astype(slc_k.dtype), new_kblk[bi, nidx, :, gi])
        vb = jnp.where(in_past, pool_vblk[phys, :, gi].astype(slc_v.dtype), new_vblk[bi, nidx, :, gi])
        return slc_attend(q, pos, idx, kb, vb, slopes)

    win_buf = cache_win_k.shape[1]
    keys = jnp.concatenate([cache_win_k.astype(win_k.dtype), win_k], axis=1)
    vals = jnp.concatenate([cache_win_v.astype(win_v.dtype), win_v], axis=1)
    pk = past - win_buf + jnp.arange(win_buf + S, dtype=jnp.int32)

    def win_fn(q, pos):
        return band_attend(q, pos, keys, vals, pk, slopes)

    side = (kc, vc, n_past_blocks + n_new_blocks, slc_fn, win_fn)
    return side, keys[:, -win_buf:], vals[:, -win_buf:]


def nsa_mixer(h, pos, side, w_qg, w_o, slopes):
    kc, vc, n_blocks, slc_fn, win_fn = side
    B, T = h.shape[:2]
    hd = N_HEADS * HEAD_DIM
    qg = h @ w_qg
    q = qg[..., :hd].reshape(B, T, N_HEADS, HEAD_DIM)
    gate = jax.nn.sigmoid(qg[..., hd:].astype(F32)).reshape(B, T, N_BRANCH, N_HEADS, 1)
    o_cmp, p_cmp = cmp_attention(q, pos, kc, vc, slopes)
    idx = select_blocks(p_cmp, pos, n_blocks)
    o_slc = slc_fn(q, pos, idx)
    o_win = win_fn(q, pos)
    o = gate[:, :, 0] * o_cmp + gate[:, :, 1] * o_slc + gate[:, :, 2] * o_win
    return o.reshape(B, T, hd).astype(h.dtype) @ w_o


def setup_inputs(seed: int = 0) -> dict:
    key = jax.random.key(seed)
    keys = iter(jax.random.split(key, 48))

    def nrm(shape, scale):
        return jax.random.normal(next(keys), shape, F32) * scale

    def gain(shape):
        return 1.0 + nrm(shape, 0.02)

    n_pages = PAST_LEN // PAGE_SIZE
    n_pool = (5 * DEC_BATCH * n_pages + 3) // 4
    win_buf = min(WINDOW, PAST_LEN)
    hd = N_HEADS * HEAD_DIM
    row = (KV_GROUPS, HEAD_DIM)
    inp = {}
    inp['x_prompt'] = nrm((BATCH, SEQ, D_MODEL), 1.0)
    inp['x_sample'] = nrm((DEC_BATCH, DEC_SEQ, D_MODEL), 1.0)
    inp['state_conv_a'] = nrm((N_A_LAYERS, DEC_BATCH, CONV_K - 1, D_MODEL), 0.5)
    inp['state_ffn_conv'] = nrm((DEPTH, DEC_BATCH, FFN_K - 1, 2 * D_FF), 1.0)
    inp['cache_cmp_k'] = nrm((n_pool, PAGE_SIZE) + row, 1.0)
    inp['cache_cmp_v'] = nrm((n_pool, PAGE_SIZE) + row, 1.0)
    inp['cache_slc_k'] = nrm((n_pool, PAGE_SIZE) + row, 1.0)
    inp['cache_slc_v'] = nrm((n_pool, PAGE_SIZE) + row, 1.0)
    inp['cache_win_k'] = nrm((DEC_BATCH, win_buf) + row, 1.0)
    inp['cache_win_v'] = nrm((DEC_BATCH, win_buf) + row, 1.0)
    perm = jax.random.permutation(next(keys), n_pool)
    inp['page_table'] = perm[:DEC_BATCH * n_pages].reshape(DEC_BATCH, n_pages).astype(jnp.int32)
    inp['g_attn'] = gain((DEPTH, D_MODEL))
    inp['g_ffn'] = gain((DEPTH, D_MODEL))
    inp['g_final'] = gain((D_MODEL,))
    inp['w_pw1'] = nrm((N_A_LAYERS, D_MODEL, 2 * D_MODEL), D_MODEL ** -0.5)
    inp['w_dw_a'] = nrm((N_A_LAYERS, CONV_K, D_MODEL), CONV_K ** -0.5)
    inp['ln_a_g'] = gain((N_A_LAYERS, D_MODEL))
    inp['ln_a_b'] = nrm((N_A_LAYERS, D_MODEL), 0.02)
    inp['w_pw2'] = nrm((N_A_LAYERS, D_MODEL, D_MODEL), D_MODEL ** -0.5)
    inp['g_kv'] = gain((D_MODEL,))
    inp['w_kv'] = nrm((D_MODEL, N_KV_PROJ * KV_GROUPS * HEAD_DIM), D_MODEL ** -0.5)
    inp['cmp_pe_k'] = nrm((CMP_BLOCK, HEAD_DIM), 0.1)
    inp['cmp_w1_k'] = nrm((CMP_BLOCK, HEAD_DIM, CMP_HID), (CMP_BLOCK * HEAD_DIM) ** -0.5)
    inp['cmp_w2_k'] = nrm((CMP_HID, HEAD_DIM), CMP_HID ** -0.5)
    inp['cmp_pe_v'] = nrm((CMP_BLOCK, HEAD_DIM), 0.1)
    inp['cmp_w1_v'] = nrm((CMP_BLOCK, HEAD_DIM, CMP_HID), (CMP_BLOCK * HEAD_DIM) ** -0.5)
    inp['cmp_w2_v'] = nrm((CMP_HID, HEAD_DIM), CMP_HID ** -0.5)
    inp['w_qg'] = nrm((N_B_LAYERS, D_MODEL, hd + N_BRANCH * N_HEADS), D_MODEL ** -0.5)
    inp['w_o'] = nrm((N_B_LAYERS, hd, D_MODEL), hd ** -0.5)
    inp['w_up'] = nrm((DEPTH, D_MODEL, 2 * D_FF), D_MODEL ** -0.5)
    inp['w_dw_f'] = nrm((DEPTH, FFN_K, 2 * D_FF), FFN_K ** -0.5)
    inp['w_down'] = nrm((DEPTH, D_FF, D_MODEL), D_FF ** -0.5)
    return inp


def reference(x_prompt, x_sample, state_conv_a, state_ffn_conv, cache_cmp_k, cache_cmp_v,
              cache_slc_k, cache_slc_v, cache_win_k, cache_win_v, page_table,
              g_attn, g_ffn, g_final, w_pw1, w_dw_a, ln_a_g, ln_a_b, w_pw2,
              g_kv, w_kv, cmp_pe_k, cmp_w1_k, cmp_w2_k, cmp_pe_v, cmp_w1_v, cmp_w2_v,
              w_qg, w_o, w_up, w_dw_f, w_down):
    slopes = alibi_slopes()
    bp, tp = x_prompt.shape[:2]
    past_len = page_table.shape[1] * PAGE_SIZE
    pos_p = jnp.arange(tp, dtype=jnp.int32)
    pos_s = past_len + jnp.arange(x_sample.shape[1], dtype=jnp.int32)
    xp, xs = x_prompt, x_sample
    conv_p, conv_s, ffn_p, ffn_s = [], [], [], []
    for layer in range(DEPTH):
        hp = rms_norm(xp, g_attn[layer])
        hs = rms_norm(xs, g_attn[layer])
        if layer < N_A_LAYERS:
            a = layer
            zero_past = jnp.zeros((bp, CONV_K - 1, D_MODEL), xp.dtype)
            yp, cp = conformer_conv(hp, zero_past, w_pw1[a], w_dw_a[a], ln_a_g[a], ln_a_b[a], w_pw2[a])
            ys, cs = conformer_conv(hs, state_conv_a[a], w_pw1[a], w_dw_a[a], ln_a_g[a], ln_a_b[a], w_pw2[a])
            conv_p.append(cp)
            conv_s.append(cs)
        else:
            b = layer - N_A_LAYERS
            yp = nsa_mixer(hp, pos_p, side_p, w_qg[b], w_o[b], slopes)
            ys = nsa_mixer(hs, pos_s, side_s, w_qg[b], w_o[b], slopes)
        xp = xp + yp
        xs = xs + ys
        zero_ffn = jnp.zeros((bp, FFN_K - 1, 2 * D_FF), xp.dtype)
        fyp, fp = conv_ffn(rms_norm(xp, g_ffn[layer]), zero_ffn, w_up[layer], w_dw_f[layer], w_down[layer])
        fys, fs = conv_ffn(rms_norm(xs, g_ffn[layer]), state_ffn_conv[layer], w_up[layer], w_dw_f[layer], w_down[layer])
        xp = xp + fyp
        xs = xs + fys
        ffn_p.append(fp)
        ffn_s.append(fs)
        if layer == N_A_LAYERS - 1:
            cmp_k_p, cmp_v_p, slc_k_p, slc_v_p, win_k_p, win_v_p = shared_kv(xp, g_kv, w_kv)
            cmp_k_s, cmp_v_s, slc_k_s, slc_v_s, win_k_s, win_v_s = shared_kv(xs, g_kv, w_kv)
            side_p, new_win_k_p, new_win_v_p = make_prompt_side(
                cmp_k_p, cmp_v_p, slc_k_p, slc_v_p, win_k_p, win_v_p,
                cmp_pe_k, cmp_w1_k, cmp_w2_k, cmp_pe_v, cmp_w1_v, cmp_w2_v, slopes)
            side_s, new_win_k_s, new_win_v_s = make_sample_side(
                cmp_k_s, cmp_v_s, slc_k_s, slc_v_s, win_k_s, win_v_s,
                cache_cmp_k, cache_cmp_v, cache_slc_k, cache_slc_v, cache_win_k, cache_win_v, page_table,
                cmp_pe_k, cmp_w1_k, cmp_w2_k, cmp_pe_v, cmp_w1_v, cmp_w2_v, slopes)
    y_prompt = rms_norm(xp, g_final)
    y_sample = rms_norm(xs, g_final)
    conv_a_prompt = jnp.stack(conv_p)
    conv_a_sample = jnp.stack(conv_s)
    ffn_prompt = jnp.stack(ffn_p)
    ffn_sample = jnp.stack(ffn_s)
    return (y_prompt, y_sample, conv_a_prompt, conv_a_sample, ffn_prompt, ffn_sample,
            cmp_k_p, cmp_v_p, slc_k_p, slc_v_p, cmp_k_s, cmp_v_s, slc_k_s, slc_v_s,
            new_win_k_p, new_win_v_p, new_win_k_s, new_win_v_s)
```

```python
import functools

import jax
import jax.numpy as jnp
from jax import lax
from jax.experimental import pallas as pl
from jax.experimental.pallas import tpu as pltpu

F32 = jnp.float32
BF16 = jnp.bfloat16

HEAD_DIM = 128
KV_GROUPS = 4
N_BRANCH = 3
CMP_BLOCK = 32
CMP_STRIDE = 16
SEL_BLOCK = 64
N_SELECT = 16
N_INIT_BLOCKS = 1
N_LOCAL_BLOCKS = 2
WINDOW = 512
PAGE_SIZE = 128
EPS = 1e-6
NEG = -1e30
FORCE = 1e9

LANES = 128
BF16_ROWS = 16
MIB = 1024 * 1024


def _cparams(sem, vmem_mib=48):
    return pltpu.CompilerParams(dimension_semantics=sem, vmem_limit_bytes=vmem_mib * MIB)


def _sds(shape, dtype):
    return jax.ShapeDtypeStruct(shape, dtype)


def _silu(x):
    return x * jax.nn.sigmoid(x)


def _rmsnorm_body(x_ref, g_ref, o_ref):
    x = x_ref[...]
    y = x * lax.rsqrt(jnp.mean(x * x, axis=-1, keepdims=True) + EPS)
    o_ref[...] = (y * g_ref[...]).astype(o_ref.dtype)


def rmsnorm(x, g, out_dtype, bm=512):
    m, d = x.shape
    bm = min(bm, m)
    return pl.pallas_call(
        _rmsnorm_body,
        out_shape=_sds((m, d), out_dtype),
        grid=(m // bm,),
        in_specs=[pl.BlockSpec((bm, d), lambda i: (i, 0)),
                  pl.BlockSpec((1, d), lambda i: (0, 0))],
        out_specs=pl.BlockSpec((bm, d), lambda i: (i, 0)),
        compiler_params=_cparams(("parallel",)),
        name="rmsnorm",
    )(x, g.reshape(1, d))


def _mm_body(*refs, glu, has_resid, scale, act, n_out):
    x_ref = refs[0]
    pos = 1
    x = x_ref[...]
    acc = jnp.dot(x, refs[pos][...], preferred_element_type=F32)
    pos += 1
    if glu:
        gate = jnp.dot(x, refs[pos][...], preferred_element_type=F32)
        pos += 1
        acc = acc * jax.nn.sigmoid(gate)
    if scale is not None:
        acc = acc * scale
    if act == "sigmoid":
        acc = jax.nn.sigmoid(acc)
    if has_resid:
        acc = acc + refs[pos][...]
        pos += 1
    for o_ref in refs[pos:pos + n_out]:
        o_ref[...] = acc.astype(o_ref.dtype)


def mm(x, w, *, bm, bn, glu=False, resid=None, scale=None, act=None,
       out_dtypes=(F32,), vmem_mib=48, name="mm"):
    m, k = x.shape
    n = w.shape[1] // 2 if glu else w.shape[1]
    bm = min(bm, m)
    bn = min(bn, n)
    assert m % bm == 0 and n % bn == 0
    in_specs = [pl.BlockSpec((bm, k), lambda i, j: (i, 0)),
                pl.BlockSpec((k, bn), lambda i, j: (0, j))]
    args = [x, w]
    if glu:
        nb = n // bn
        in_specs.append(pl.BlockSpec((k, bn), lambda i, j: (0, j + nb)))
        args.append(w)
    if resid is not None:
        in_specs.append(pl.BlockSpec((bm, bn), lambda i, j: (i, j)))
        args.append(resid)
    outs = pl.pallas_call(
        functools.partial(_mm_body, glu=glu, has_resid=resid is not None, scale=scale,
                          act=act, n_out=len(out_dtypes)),
        out_shape=[_sds((m, n), dt) for dt in out_dtypes],
        grid=(m // bm, n // bn),
        in_specs=in_specs,
        out_specs=[pl.BlockSpec((bm, bn), lambda i, j: (i, j)) for _ in out_dtypes],
        compiler_params=_cparams(("parallel", "arbitrary"), vmem_mib),
        name=name,
    )(*args)
    return outs[0] if len(out_dtypes) == 1 else outs


def _ln_silu(c, g, b):
    mu = jnp.mean(c, axis=-1, keepdims=True)
    xc = c - mu
    y = xc * lax.rsqrt(jnp.mean(xc * xc, axis=-1, keepdims=True) + EPS)
    return _silu(y * g + b)


CONV_HALO = 32


SUBLANES = 8


def _conv_ln_body(u_ref, halo_ref, w_ref, g_ref, b_ref, o_ref, xs_ref, sh_ref, c_ref,
                  *, bt, blocks_per_seq, kw, cw, rc):
    i = pl.program_id(0)
    d = u_ref.shape[1]
    keep = jnp.where(i % blocks_per_seq == 0, 0.0, 1.0)
    xs_ref[0:CONV_HALO, :] = halo_ref[...] * keep
    xs_ref[CONV_HALO:, :] = u_ref[...]
    base = CONV_HALO - (kw - 1)
    span = sh_ref.shape[1]

    def chunk(ci, carry):
        lanes = pl.ds(pl.multiple_of(ci * cw, cw), cw)
        for r in range(1, SUBLANES):
            sh_ref[r - 1] = xs_ref[r:r + span, lanes]
        for rb in range(bt // rc):
            acc = jnp.zeros((rc, cw), F32)
            for k in range(kw):
                a, r = divmod(base + k, SUBLANES)
                row0 = rb * rc + SUBLANES * a
                if r == 0:
                    x = xs_ref[row0:row0 + rc, lanes]
                else:
                    x = sh_ref[r - 1, row0:row0 + rc, :]
                acc = acc + w_ref[k:k + 1, lanes] * x
            c_ref[rb * rc:(rb + 1) * rc, lanes] = acc
        return carry

    lax.fori_loop(0, d // cw, chunk, 0)

    def ln(rb, carry):
        rows = pl.ds(pl.multiple_of(rb * rc, rc), rc)
        o_ref[rows, :] = _ln_silu(c_ref[rows, :], g_ref[...], b_ref[...]).astype(o_ref.dtype)
        return carry

    lax.fori_loop(0, bt // rc, ln, 0)


def conv_ln_prompt(u, seq, w_dw, ln_g, ln_b, bt=256, cw=512, rc=64):
    m, d = u.shape
    kw = w_dw.shape[0]
    assert kw - 1 <= CONV_HALO and seq % bt == 0 and bt % CONV_HALO == 0
    assert d % cw == 0 and bt % rc == 0
    r = bt // CONV_HALO
    span = bt + CONV_HALO - SUBLANES
    return pl.pallas_call(
        functools.partial(_conv_ln_body, bt=bt, blocks_per_seq=seq // bt, kw=kw, cw=cw, rc=rc),
        out_shape=_sds((m, d), BF16),
        grid=(m // bt,),
        in_specs=[pl.BlockSpec((bt, d), lambda i: (i, 0)),
                  pl.BlockSpec((CONV_HALO, d), lambda i: (jnp.maximum(i * r - 1, 0), 0)),
                  pl.BlockSpec((kw, d), lambda i: (0, 0)),
                  pl.BlockSpec((1, d), lambda i: (0, 0)),
                  pl.BlockSpec((1, d), lambda i: (0, 0))],
        out_specs=pl.BlockSpec((bt, d), lambda i: (i, 0)),
        scratch_shapes=[pltpu.VMEM((bt + CONV_HALO, d), F32),
                        pltpu.VMEM((SUBLANES - 1, span, cw), F32),
                        pltpu.VMEM((bt, d), F32)],
        compiler_params=_cparams(("parallel",)),
        name="conv_ln_prompt",
    )(u, u, w_dw, ln_g.reshape(1, d), ln_b.reshape(1, d))


def _conv_ln_sample_body(xx_ref, w_ref, g_ref, b_ref, o_ref, *, s, kw):
    acc = jnp.zeros((s, xx_ref.shape[-1]), F32)
    for k in range(kw):
        acc = acc + w_ref[k:k + 1, :] * xx_ref[k:k + s, :]
    o_ref[...] = _ln_silu(acc, g_ref[...], b_ref[...])


def conv_ln_sample(xx, s, w_dw, ln_g, ln_b):
    b, l, d = xx.shape
    kw = w_dw.shape[0]
    return pl.pallas_call(
        functools.partial(_conv_ln_sample_body, s=s, kw=kw),
        out_shape=_sds((b, s, d), F32),
        grid=(b,),
        in_specs=[pl.BlockSpec((None, l, d), lambda i: (i, 0, 0)),
                  pl.BlockSpec((kw, d), lambda i: (0, 0)),
                  pl.BlockSpec((1, d), lambda i: (0, 0)),
                  pl.BlockSpec((1, d), lambda i: (0, 0))],
        out_specs=pl.BlockSpec((None, s, d), lambda i: (i, 0, 0)),
        compiler_params=_cparams(("parallel",)),
        name="conv_ln_sample",
    )(xx, w_dw, ln_g.reshape(1, d), ln_b.reshape(1, d))


FFN_HALO = BF16_ROWS
STATE_ROWS = 8


def _ffn_up_body(h_ref, halo_ref, wa_ref, wb_ref, dwa_ref, dwb_ref, act_ref, sta_ref, stb_ref,
                 hx_ref, *, bm, blocks_per_seq, kw):
    i = pl.program_id(0)

    @pl.when(pl.program_id(1) == 0)
    def _():
        keep = jnp.where(i % blocks_per_seq == 0, 0.0, 1.0)
        hx_ref[0:FFN_HALO, :] = (halo_ref[...].astype(F32) * keep).astype(BF16)
        hx_ref[FFN_HALO:, :] = h_ref[...]

    hx = hx_ref[...]
    ua = jnp.dot(hx, wa_ref[...], preferred_element_type=F32)
    ub = jnp.dot(hx, wb_ref[...], preferred_element_type=F32)

    def conv(u, dw_ref):
        c = jnp.zeros((bm, u.shape[1]), F32)
        for k in range(kw):
            off = FFN_HALO - (kw - 1) + k
            c = c + dw_ref[k:k + 1, :] * u[off:off + bm, :]
        return c

    ca = conv(ua, dwa_ref)
    cb = conv(ub, dwb_ref)
    act_ref[...] = (_silu(ca) * cb).astype(act_ref.dtype)
    sta_ref[...] = ua[FFN_HALO + bm - STATE_ROWS:, :]
    stb_ref[...] = ub[FFN_HALO + bm - STATE_ROWS:, :]


def ffn_up_prompt(h, seq, w_up, w_dw, bm=1024, bn=256, vmem_mib=48):
    m, d = h.shape
    f = w_up.shape[1] // 2
    kw = w_dw.shape[0]
    nbatch = m // seq
    assert seq % bm == 0 and f % bn == 0 and kw - 1 <= FFN_HALO
    bps = seq // bm
    r = bm // FFN_HALO
    nb = f // bn
    act, sta, stb = pl.pallas_call(
        functools.partial(_ffn_up_body, bm=bm, blocks_per_seq=bps, kw=kw),
        out_shape=[_sds((m, f), BF16), _sds((m // bm, STATE_ROWS, f), F32),
                   _sds((m // bm, STATE_ROWS, f), F32)],
        grid=(m // bm, nb),
        in_specs=[pl.BlockSpec((bm, d), lambda i, j: (i, 0)),
                  pl.BlockSpec((FFN_HALO, d), lambda i, j: (jnp.maximum(i * r - 1, 0), 0)),
                  pl.BlockSpec((d, bn), lambda i, j: (0, j)),
                  pl.BlockSpec((d, bn), lambda i, j: (0, j + nb)),
                  pl.BlockSpec((kw, bn), lambda i, j: (0, j)),
                  pl.BlockSpec((kw, bn), lambda i, j: (0, j + nb))],
        out_specs=[pl.BlockSpec((bm, bn), lambda i, j: (i, j)),
                   pl.BlockSpec((None, STATE_ROWS, bn), lambda i, j: (i, 0, j)),
                   pl.BlockSpec((None, STATE_ROWS, bn), lambda i, j: (i, 0, j))],
        scratch_shapes=[pltpu.VMEM((bm + FFN_HALO, d), BF16)],
        compiler_params=_cparams(("parallel", "arbitrary"), vmem_mib),
        name="ffn_up_prompt",
    )(h, h, w_up, w_up, w_dw, w_dw)
    return act, sta[bps - 1::bps], stb[bps - 1::bps]


def _ffn_gate_sample_body(xa_ref, xb_ref, dwa_ref, dwb_ref, o_ref, *, s, kw):
    def conv(x_ref, dw_ref):
        c = jnp.zeros((x_ref.shape[0], s, x_ref.shape[2]), F32)
        for k in range(kw):
            c = c + dw_ref[k:k + 1, :] * x_ref[:, k:k + s, :]
        return c

    o_ref[...] = _silu(conv(xa_ref, dwa_ref)) * conv(xb_ref, dwb_ref)


def ffn_gate_sample(xx, s, w_dw):
    b, l, f2 = xx.shape
    f = f2 // 2
    kw = w_dw.shape[0]
    return pl.pallas_call(
        functools.partial(_ffn_gate_sample_body, s=s, kw=kw),
        out_shape=_sds((b, s, f), F32),
        grid=(1,),
        in_specs=[pl.BlockSpec((b, l, f), lambda i: (0, 0, 0)),
                  pl.BlockSpec((b, l, f), lambda i: (0, 0, 1)),
                  pl.BlockSpec((kw, f), lambda i: (0, 0)),
                  pl.BlockSpec((kw, f), lambda i: (0, 1))],
        out_specs=pl.BlockSpec((b, s, f), lambda i: (0, 0, 0)),
        compiler_params=_cparams(("arbitrary",)),
        name="ffn_gate_sample",
    )(xx, xx, w_dw, w_dw)


def _cmp_proj_body(*refs, n_x, n_prefetch=0):
    refs = refs[n_prefetch:]
    x_refs = refs[:n_x]
    pe_ref, w_ref, lo_ref, hi_ref = refs[n_x:]
    x = x_refs[0][...] if n_x == 1 else jnp.concatenate([r[...] for r in x_refs], axis=0)
    lo_ref[...] = jnp.dot((x + pe_ref[0:1, :]).astype(BF16), w_ref[0], preferred_element_type=F32)
    hi_ref[...] = jnp.dot((x + pe_ref[1:2, :]).astype(BF16), w_ref[1], preferred_element_type=F32)


def cmp_proj(x, pe, w, bm=128):
    r, kk = x.shape
    n = w.shape[2]
    bm = min(bm, r)
    return pl.pallas_call(
        functools.partial(_cmp_proj_body, n_x=1),
        out_shape=[_sds((r, n), F32), _sds((r, n), F32)],
        grid=(r // bm,),
        in_specs=[pl.BlockSpec((bm, kk), lambda i: (i, 0)),
                  pl.BlockSpec((2, kk), lambda i: (0, 0)),
                  pl.BlockSpec((2, kk, n), lambda i: (0, 0, 0))],
        out_specs=[pl.BlockSpec((bm, n), lambda i: (i, 0)), pl.BlockSpec((bm, n), lambda i: (i, 0))],
        compiler_params=_cparams(("parallel",), 56),
        name="cmp_proj",
    )(x, pe, w)


def cmp_proj_paged(pool, page_ids, pe, w, pages_per_step=16):
    _, hp, kk = pool.shape
    n = w.shape[2]
    npg = page_ids.shape[0]
    pps = pages_per_step
    assert npg % pps == 0
    bm = hp * pps

    def page_spec(p):
        return pl.BlockSpec((None, hp, kk), lambda i, ids: (ids[i * pps + p], 0, 0))

    grid_spec = pltpu.PrefetchScalarGridSpec(
        num_scalar_prefetch=1,
        grid=(npg // pps,),
        in_specs=[page_spec(p) for p in range(pps)] + [
            pl.BlockSpec((2, kk), lambda i, ids: (0, 0)),
            pl.BlockSpec((2, kk, n), lambda i, ids: (0, 0, 0))],
        out_specs=[pl.BlockSpec((bm, n), lambda i, ids: (i, 0)),
                   pl.BlockSpec((bm, n), lambda i, ids: (i, 0))],
    )
    return pl.pallas_call(
        functools.partial(_cmp_proj_body, n_x=pps, n_prefetch=1),
        out_shape=[_sds((npg * hp, n), F32), _sds((npg * hp, n), F32)],
        grid_spec=grid_spec,
        compiler_params=_cparams(("arbitrary",), 56),
        name="cmp_proj_paged",
    )(page_ids, *([pool] * pps), pe, w)


def _cmp_out_body(lo_ref, hi_ref, w2_ref, o_ref, ob_ref):
    s = _silu(lo_ref[...] + hi_ref[...])
    for g in range(KV_GROUPS):
        sl = slice(g * HEAD_DIM, (g + 1) * HEAD_DIM)
        y = jnp.dot(s[:, sl].astype(BF16), w2_ref[...], preferred_element_type=F32)
        o_ref[:, sl] = y
        ob_ref[:, sl] = y.astype(BF16)


def cmp_out(lo, hi_next, w2, bn=128):
    b, nc, n = lo.shape
    bn = min(bn, nc)
    spec = pl.BlockSpec((None, bn, n), lambda i, j: (i, j, 0))
    return pl.pallas_call(
        _cmp_out_body,
        out_shape=[_sds((b, nc, n), F32), _sds((b, nc, n), BF16)],
        grid=(b, nc // bn),
        in_specs=[spec, spec, pl.BlockSpec((HEAD_DIM, HEAD_DIM), lambda i, j: (0, 0))],
        out_specs=[spec, spec],
        compiler_params=_cparams(("parallel", "parallel")),
        name="cmp_out",
    )(lo, hi_next, w2)


def _cmp_weights(pe, w1):
    w12 = w1.reshape(2, CMP_STRIDE, HEAD_DIM, -1)
    hid = w12.shape[-1]
    eye = jnp.eye(KV_GROUPS, dtype=F32)
    wbig = jnp.einsum("sjde,gh->sjgdhe", w12, eye).reshape(
        2, CMP_STRIDE * KV_GROUPS * HEAD_DIM, KV_GROUPS * hid).astype(BF16)
    pe2 = pe.reshape(2, CMP_STRIDE, 1, HEAD_DIM)
    pebig = jnp.broadcast_to(pe2, (2, CMP_STRIDE, KV_GROUPS, HEAD_DIM)).reshape(2, -1)
    return pebig, wbig


def _iota(shape, dim):
    return lax.broadcasted_iota(jnp.int32, shape, dim)


def _dot_nt(a, b):
    return lax.dot_general(a, b, (((1,), (1,)), ((), ())), preferred_element_type=F32)


def _split3_dot(p, c):
    p1 = p.astype(BF16)
    r1 = p - p1.astype(F32)
    p2 = r1.astype(BF16)
    p3 = (r1 - p2.astype(F32)).astype(BF16)
    return (jnp.dot(p1, c, preferred_element_type=F32) + jnp.dot(p2, c, preferred_element_type=F32)
            + jnp.dot(p3, c, preferred_element_type=F32))


def _cover(n_cmp, n_sel):
    n = _iota((n_cmp, n_sel), 0)
    j = _iota((n_cmp, n_sel), 1)
    lo = jnp.maximum(n * CMP_STRIDE, j * SEL_BLOCK)
    hi = jnp.minimum(n * CMP_STRIDE + CMP_BLOCK, (j + 1) * SEL_BLOCK)
    return (jnp.maximum(hi - lo, 0).astype(F32) * (1.0 / CMP_STRIDE)).astype(BF16)


def _head_slopes(g, hpg, shape, dim):
    h = g * hpg + _iota(shape, dim) + 1
    return jnp.exp2(h.astype(F32) * (-8.0 / (KV_GROUPS * hpg)))


def _flash_step(s3, v, m_ref, l_ref, acc_ref):
    hh, tq, tk = s3.shape
    m_prev = m_ref[...]
    m_new = jnp.maximum(m_prev, jnp.max(s3, axis=-1, keepdims=True))
    alpha = jnp.exp(m_prev - m_new)
    p = jnp.exp(s3 - m_new)
    l_ref[...] = alpha * l_ref[...] + jnp.sum(p, axis=-1, keepdims=True)
    pv = jnp.dot(p.reshape(hh * tq, tk).astype(BF16), v, preferred_element_type=F32)
    acc_ref[...] = alpha * acc_ref[...] + pv.reshape(hh, tq, HEAD_DIM)
    m_ref[...] = m_new


def _flash_init(m_ref, l_ref, acc_ref):
    m_ref[...] = jnp.full(m_ref.shape, NEG, F32)
    l_ref[...] = jnp.zeros(l_ref.shape, F32)
    acc_ref[...] = jnp.zeros(acc_ref.shape, F32)


def _select_rank(score, idx, n_candidates):
    rank = jnp.zeros(score.shape, jnp.int32)
    for i in range(n_candidates):
        si = score[:, i:i + 1]
        before = (si > score) | ((si == score) & (i < idx))
        rank = rank + before.astype(jnp.int32)
    return rank


def _nsa_prompt_body(q_ref, gate_ref, kc_ref, vc_ref, sk_ref, sv_ref, wk_ref, wv_ref, o_ref,
                     qs_ref, m_ref, l_ref, acc_ref, *, tq, tk, seq, n_cmp, hpg):
    g = pl.program_id(1)
    i = pl.program_id(2)
    t0 = i * tq
    n_sel = seq // SEL_BLOCK
    n_pick = min(N_SELECT, n_sel)
    rows = hpg * tq

    for h in range(hpg):
        qs_ref[h * tq:(h + 1) * tq, :] = q_ref[:, h * HEAD_DIM:(h + 1) * HEAD_DIM]
    qs = qs_ref[...]
    slope3 = _head_slopes(g, hpg, (hpg, 1, 1), 0)
    t_col = t0 + _iota((tq, 1), 0)

    ncp = kc_ref.shape[0]
    s3 = _dot_nt(qs, kc_ref[...]).reshape(hpg, tq, ncp)
    n_row = _iota((1, ncp), 1)
    dist = t_col - (n_row * CMP_STRIDE + (CMP_BLOCK - 1))
    valid = (dist >= 0) & (n_row < n_cmp)
    s3 = jnp.where(valid[None], s3 - slope3 * dist.astype(F32)[None], NEG)
    e = jnp.exp(s3 - jnp.max(s3, axis=-1, keepdims=True))
    p3 = e / jnp.sum(e, axis=-1, keepdims=True) * valid.astype(F32)[None]
    o_cmp = jnp.dot(p3.reshape(rows, ncp).astype(BF16), vc_ref[...],
                    preferred_element_type=F32).reshape(hpg, tq, HEAD_DIM)

    psum = jnp.sum(p3, axis=0)
    score = _split3_dot(psum, _cover(ncp, LANES))
    j_row = _iota((1, LANES), 1)
    cur = t_col // SEL_BLOCK
    forced = (j_row < N_INIT_BLOCKS) | ((cur - j_row >= 0) & (cur - j_row < N_LOCAL_BLOCKS))
    score = jnp.where(forced, FORCE, score)
    score = jnp.where(j_row * SEL_BLOCK <= t_col, score, NEG)
    score = jnp.where(j_row < n_sel, score, -jnp.inf)
    sel = (_select_rank(score, j_row, n_sel) < n_pick) & (j_row < n_sel)
    sel_b = sel.astype(BF16)

    _flash_init(m_ref, l_ref, acc_ref)

    def slc_step(kt, carry):
        k0 = pl.multiple_of(kt * tk, tk)
        kpos = k0 + _iota((1, tk), 1)
        expand = (_iota((LANES, tk), 0) == (k0 + _iota((LANES, tk), 1)) // SEL_BLOCK).astype(BF16)
        picked = jnp.dot(sel_b, expand, preferred_element_type=F32)
        d = t_col - kpos
        ok = (picked > 0.5) & (d >= 0)
        s = _dot_nt(qs, sk_ref[pl.ds(k0, tk), :]).reshape(hpg, tq, tk)
        s = jnp.where(ok[None], s - slope3 * d.astype(F32)[None], NEG)
        _flash_step(s, sv_ref[pl.ds(k0, tk), :], m_ref, l_ref, acc_ref)
        return carry

    lax.fori_loop(0, (t0 + tq + tk - 1) // tk, slc_step, 0)
    o_slc = acc_ref[...] / l_ref[...]

    _flash_init(m_ref, l_ref, acc_ref)

    def win_step(kt, carry):
        k0 = pl.multiple_of(kt * tk, tk)
        d = t_col - (k0 + _iota((1, tk), 1))
        ok = (d >= 0) & (d < WINDOW)
        s = _dot_nt(qs, wk_ref[pl.ds(k0, tk), :]).reshape(hpg, tq, tk)
        s = jnp.where(ok[None], s - slope3 * d.astype(F32)[None], NEG)
        _flash_step(s, wv_ref[pl.ds(k0, tk), :], m_ref, l_ref, acc_ref)
        return carry

    lax.fori_loop(jnp.maximum(t0 - WINDOW + 1, 0) // tk, (t0 + tq + tk - 1) // tk, win_step, 0)
    o_win = acc_ref[...] / l_ref[...]

    gate = gate_ref[...]
    for h in range(hpg):
        def gcol(br):
            c = br * hpg + h
            return gate[:, c:c + 1]
        o = gcol(0) * o_cmp[h] + gcol(1) * o_slc[h] + gcol(2) * o_win[h]
        o_ref[:, h * HEAD_DIM:(h + 1) * HEAD_DIM] = o.astype(o_ref.dtype)


def nsa_prompt(q, gate, kc, vc, kvb, batch, seq, n_cmp, tq=256, tk=256):
    m, hd = q.shape
    hpg = hd // HEAD_DIM // KV_GROUPS
    gw = hpg * HEAD_DIM
    ncp = kc.shape[1]
    nq = seq // tq
    assert seq % tq == 0 and seq % tk == 0 and seq // SEL_BLOCK <= LANES and tq % tk == 0

    def kv_spec(proj):
        return pl.BlockSpec((seq, HEAD_DIM), lambda b, g, i: (b, proj * KV_GROUPS + g))

    return pl.pallas_call(
        functools.partial(_nsa_prompt_body, tq=tq, tk=tk, seq=seq, n_cmp=n_cmp, hpg=hpg),
        out_shape=_sds((m, hd), BF16),
        grid=(batch, KV_GROUPS, nq),
        in_specs=[pl.BlockSpec((tq, gw), lambda b, g, i: (b * nq + i, g)),
                  pl.BlockSpec((tq, LANES), lambda b, g, i: (b * nq + i, g)),
                  pl.BlockSpec((None, ncp, HEAD_DIM), lambda b, g, i: (b, 0, g)),
                  pl.BlockSpec((None, ncp, HEAD_DIM), lambda b, g, i: (b, 0, g)),
                  kv_spec(2), kv_spec(3), kv_spec(4), kv_spec(5)],
        out_specs=pl.BlockSpec((tq, gw), lambda b, g, i: (b * nq + i, g)),
        scratch_shapes=[pltpu.VMEM((hpg * tq, HEAD_DIM), BF16),
                        pltpu.VMEM((hpg, tq, 1), F32),
                        pltpu.VMEM((hpg, tq, 1), F32),
                        pltpu.VMEM((hpg, tq, HEAD_DIM), F32)],
        compiler_params=_cparams(("parallel", "parallel", "arbitrary")),
        name="nsa_prompt",
    )(q, gate, kc, vc, kvb, kvb, kvb, kvb)


def _stack_heads(q, g, hpg):
    return jnp.concatenate(
        [q[:, (g * hpg + h) * HEAD_DIM:(g * hpg + h + 1) * HEAD_DIM] for h in range(hpg)],
        axis=0).astype(BF16)


def _nsa_sample_cmp_win_body(q_ref, kc_ref, vc_ref, cwk_ref, cwv_ref, nwk_ref, nwv_ref,
                             ocmp_ref, owin_ref, sel_ref, *, s, past, n_cmp, n_sel, hpg):
    q = q_ref[...].astype(F32)
    ncp = kc_ref.shape[0]
    nbp = sel_ref.shape[-1]
    wbuf = cwk_ref.shape[0]
    n_pick = min(N_SELECT, n_sel)
    pos = past + _iota((s, 1), 0)
    cover = _cover(ncp, nbp)
    n_row = _iota((1, ncp), 1)
    dist_c = pos - (n_row * CMP_STRIDE + (CMP_BLOCK - 1))
    valid_c = (dist_c >= 0) & (n_row < n_cmp)
    j_row = _iota((1, nbp), 1)
    cur = pos // SEL_BLOCK
    forced = (j_row < N_INIT_BLOCKS) | ((cur - j_row >= 0) & (cur - j_row < N_LOCAL_BLOCKS))
    pad_rows = LANES - s
    d_w1 = pos - (past - wbuf + _iota((1, wbuf), 1))
    ok_w1 = (d_w1 >= 0) & (d_w1 < WINDOW) & (past - wbuf + _iota((1, wbuf), 1) >= 0)
    d_w2 = pos - (past + _iota((1, LANES), 1))
    ok_w2 = (d_w2 >= 0) & (d_w2 < WINDOW) & (_iota((1, LANES), 1) < s)

    for g in range(KV_GROUPS):
        gs = slice(g * HEAD_DIM, (g + 1) * HEAD_DIM)
        qs = _stack_heads(q, g, hpg)
        slope3 = _head_slopes(g, hpg, (hpg, 1, 1), 0)

        s3 = _dot_nt(qs, kc_ref[:, gs]).reshape(hpg, s, ncp)
        s3 = jnp.where(valid_c[None], s3 - slope3 * dist_c.astype(F32)[None], NEG)
        e = jnp.exp(s3 - jnp.max(s3, axis=-1, keepdims=True))
        p3 = e / jnp.sum(e, axis=-1, keepdims=True) * valid_c.astype(F32)[None]
        ocmp_ref[g] = jnp.dot(p3.reshape(hpg * s, ncp).astype(BF16), vc_ref[:, gs],
                              preferred_element_type=F32)

        score = _split3_dot(jnp.sum(p3, axis=0), cover)
        score = jnp.where(forced, FORCE, score)
        score = jnp.where(j_row * SEL_BLOCK <= pos, score, NEG)
        score = jnp.where(j_row < n_sel, score, -jnp.inf)
        sel = (_select_rank(score, j_row, n_sel) < n_pick) & (j_row < n_sel)
        sel_ref[g] = sel.astype(F32)

        zpad = jnp.zeros((pad_rows, HEAD_DIM), F32)
        k2 = jnp.concatenate([nwk_ref[:, gs], zpad], axis=0).astype(BF16)
        v2 = jnp.concatenate([nwv_ref[:, gs], zpad], axis=0).astype(BF16)
        s1 = _dot_nt(qs, cwk_ref[:, gs].astype(BF16)).reshape(hpg, s, wbuf)
        s2 = _dot_nt(qs, k2).reshape(hpg, s, LANES)
        s1 = jnp.where(ok_w1[None], s1 - slope3 * d_w1.astype(F32)[None], NEG)
        s2 = jnp.where(ok_w2[None], s2 - slope3 * d_w2.astype(F32)[None], NEG)
        mx = jnp.maximum(jnp.max(s1, axis=-1, keepdims=True), jnp.max(s2, axis=-1, keepdims=True))
        p1 = jnp.exp(s1 - mx)
        p2 = jnp.exp(s2 - mx)
        den = jnp.sum(p1, axis=-1, keepdims=True) + jnp.sum(p2, axis=-1, keepdims=True)
        ow = (jnp.dot(p1.reshape(hpg * s, wbuf).astype(BF16), cwv_ref[:, gs].astype(BF16),
                      preferred_element_type=F32)
              + jnp.dot(p2.reshape(hpg * s, LANES).astype(BF16), v2, preferred_element_type=F32))
        owin_ref[g] = ow / den.reshape(hpg * s, 1)


def nsa_sample_cmp_win(q, kc, vc, cwk, cwv, kv_new, past, n_cmp, n_sel):
    b, s, hd = q.shape
    hpg = hd // HEAD_DIM // KV_GROUPS
    ncp = kc.shape[1]
    wbuf = cwk.shape[1]
    gd = KV_GROUPS * HEAD_DIM
    nbp = -(-n_sel // LANES) * LANES
    o_shape = _sds((b, KV_GROUPS, hpg * s, HEAD_DIM), F32)
    o_spec = pl.BlockSpec((None, KV_GROUPS, hpg * s, HEAD_DIM), lambda i: (i, 0, 0, 0))
    return pl.pallas_call(
        functools.partial(_nsa_sample_cmp_win_body, s=s, past=past, n_cmp=n_cmp, n_sel=n_sel, hpg=hpg),
        out_shape=[o_shape, o_shape, _sds((b, KV_GROUPS, s, nbp), F32)],
        grid=(b,),
        in_specs=[pl.BlockSpec((None, s, hd), lambda i: (i, 0, 0)),
                  pl.BlockSpec((None, ncp, gd), lambda i: (i, 0, 0)),
                  pl.BlockSpec((None, ncp, gd), lambda i: (i, 0, 0)),
                  pl.BlockSpec((None, wbuf, gd), lambda i: (i, 0, 0)),
                  pl.BlockSpec((None, wbuf, gd), lambda i: (i, 0, 0)),
                  pl.BlockSpec((None, s, gd), lambda i: (i, 0, 4)),
                  pl.BlockSpec((None, s, gd), lambda i: (i, 0, 5))],
        out_specs=[o_spec, o_spec,
                   pl.BlockSpec((None, KV_GROUPS, s, nbp), lambda i: (i, 0, 0, 0))],
        compiler_params=_cparams(("parallel",)),
        name="nsa_sample_cmp_win",
    )(q, kc, vc, cwk, cwv, kv_new, kv_new)


def _nsa_sample_slc_body(*refs, s, past, hpg, pps, n_chunks):
    ids_ref = refs[0]
    del ids_ref
    kp_refs = refs[1:1 + pps]
    vp_refs = refs[1 + pps:1 + 2 * pps]
    (q_ref, sel_ref, nk_ref, nv_ref, gate_ref, ocmp_ref, owin_ref, o_ref,
     qs_ref, m_ref, l_ref, acc_ref) = refs[1 + 2 * pps:]
    c = pl.program_id(1)
    nbp = sel_ref.shape[-1]
    tk = pps * PAGE_SIZE
    pos = past + _iota((s, 1), 0)

    @pl.when(c == 0)
    def _():
        q = q_ref[...].astype(F32)
        for g in range(KV_GROUPS):
            qs_ref[g] = _stack_heads(q, g, hpg)
            _flash_init(m_ref.at[g], l_ref.at[g], acc_ref.at[g])

    k0 = c * tk
    d = pos - (k0 + _iota((1, tk), 1))
    blk = (k0 + _iota((nbp, tk), 1)) // SEL_BLOCK
    expand = (_iota((nbp, tk), 0) == blk).astype(BF16)
    for g in range(KV_GROUPS):
        gs = slice(g * HEAD_DIM, (g + 1) * HEAD_DIM)
        slope3 = _head_slopes(g, hpg, (hpg, 1, 1), 0)
        kk = jnp.concatenate([r[:, gs] for r in kp_refs], axis=0).astype(BF16)
        vv = jnp.concatenate([r[:, gs] for r in vp_refs], axis=0).astype(BF16)
        picked = jnp.dot(sel_ref[g].astype(BF16), expand, preferred_element_type=F32)
        ok = (picked > 0.5) & (d >= 0)
        sc = _dot_nt(qs_ref[g], kk).reshape(hpg, s, tk)
        sc = jnp.where(ok[None], sc - slope3 * d.astype(F32)[None], NEG)
        _flash_step(sc, vv, m_ref.at[g], l_ref.at[g], acc_ref.at[g])

    @pl.when(c == n_chunks - 1)
    def _():
        jn = past // SEL_BLOCK
        d2 = pos - (past + _iota((1, LANES), 1))
        in_new = (d2 >= 0) & (_iota((1, LANES), 1) < s)
        zpad = jnp.zeros((LANES - s, HEAD_DIM), F32)
        gate = gate_ref[...]
        for g in range(KV_GROUPS):
            gs = slice(g * HEAD_DIM, (g + 1) * HEAD_DIM)
            slope3 = _head_slopes(g, hpg, (hpg, 1, 1), 0)
            k2 = jnp.concatenate([nk_ref[:, gs], zpad], axis=0).astype(BF16)
            v2 = jnp.concatenate([nv_ref[:, gs], zpad], axis=0).astype(BF16)
            ok = in_new & (sel_ref[g][:, jn:jn + 1] > 0.5)
            sc = _dot_nt(qs_ref[g], k2).reshape(hpg, s, LANES)
            sc = jnp.where(ok[None], sc - slope3 * d2.astype(F32)[None], NEG)
            _flash_step(sc, v2, m_ref.at[g], l_ref.at[g], acc_ref.at[g])
            o_slc = acc_ref[g] / l_ref[g]
            o_cmp = ocmp_ref[g].reshape(hpg, s, HEAD_DIM)
            o_win = owin_ref[g].reshape(hpg, s, HEAD_DIM)
            for h in range(hpg):
                def gcol(br):
                    col = g * LANES + br * hpg + h
                    return gate[:, col:col + 1]
                o = gcol(0) * o_cmp[h] + gcol(1) * o_slc[h] + gcol(2) * o_win[h]
                hs = (g * hpg + h) * HEAD_DIM
                o_ref[:, hs:hs + HEAD_DIM] = o


def nsa_sample_slc(q, sel, kv_new, gate, o_cmp, o_win, pool_k, pool_v, page_table, past,
                   pages_per_step=4):
    b, s, hd = q.shape
    hpg = hd // HEAD_DIM // KV_GROUPS
    gd = KV_GROUPS * HEAD_DIM
    npages = page_table.shape[1]
    pps = pages_per_step
    assert npages % pps == 0 and s <= SEL_BLOCK and past % SEL_BLOCK == 0
    n_chunks = npages // pps
    nbp = sel.shape[-1]

    def page_spec(p):
        return pl.BlockSpec((None, PAGE_SIZE, gd),
                            lambda i, c, ids: (ids[i * npages + c * pps + p], 0, 0))

    def per_b(shape):
        nd = len(shape)
        return pl.BlockSpec((None,) + shape, lambda i, c, ids: (i,) + (0,) * nd)

    o4 = (KV_GROUPS, hpg * s, HEAD_DIM)
    grid_spec = pltpu.PrefetchScalarGridSpec(
        num_scalar_prefetch=1,
        grid=(b, n_chunks),
        in_specs=[page_spec(p) for p in range(pps)] + [page_spec(p) for p in range(pps)] + [
            per_b((s, hd)), per_b((KV_GROUPS, s, nbp)),
            pl.BlockSpec((None, s, gd), lambda i, c, ids: (i, 0, 2)),
            pl.BlockSpec((None, s, gd), lambda i, c, ids: (i, 0, 3)),
            per_b((s, KV_GROUPS * LANES)), per_b(o4), per_b(o4)],
        out_specs=per_b((s, hd)),
        scratch_shapes=[pltpu.VMEM((KV_GROUPS, hpg * s, HEAD_DIM), BF16),
                        pltpu.VMEM((KV_GROUPS, hpg, s, 1), F32),
                        pltpu.VMEM((KV_GROUPS, hpg, s, 1), F32),
                        pltpu.VMEM((KV_GROUPS, hpg, s, HEAD_DIM), F32)],
    )
    return pl.pallas_call(
        functools.partial(_nsa_sample_slc_body, s=s, past=past, hpg=hpg, pps=pps, n_chunks=n_chunks),
        out_shape=_sds((b, s, hd), F32),
        grid_spec=grid_spec,
        compiler_params=_cparams(("parallel", "arbitrary")),
        name="nsa_sample_slc",
    )(page_table.reshape(-1), *([pool_k] * pps), *([pool_v] * pps),
      q, sel, kv_new, kv_new, gate, o_cmp, o_win)


def _gate_weight(w_qg, hd, hpg):
    wg = w_qg[:, hd:].reshape(-1, N_BRANCH, KV_GROUPS, hpg).transpose(0, 2, 1, 3)
    wg = wg.reshape(-1, KV_GROUPS, N_BRANCH * hpg)
    wg = jnp.pad(wg, ((0, 0), (0, 0), (0, LANES - N_BRANCH * hpg)))
    return wg.reshape(-1, KV_GROUPS * LANES).astype(BF16)


def kernel(x_prompt, x_sample, state_conv_a, state_ffn_conv, cache_cmp_k, cache_cmp_v, cache_slc_k,
           cache_slc_v, cache_win_k, cache_win_v, page_table, g_attn, g_ffn, g_final, w_pw1, w_dw_a,
           ln_a_g, ln_a_b, w_pw2, g_kv, w_kv, cmp_pe_k, cmp_w1_k, cmp_w2_k, cmp_pe_v, cmp_w1_v,
           cmp_w2_v, w_qg, w_o, w_up, w_dw_f, w_down):
    bp, tp, d = x_prompt.shape
    bs, ts, _ = x_sample.shape
    depth = g_attn.shape[0]
    n_a = w_pw1.shape[0]
    assert depth == n_a + 1, "one NSA layer after the convolution layers"
    f = w_down.shape[1]
    hd = w_o.shape[1]
    hpg = hd // HEAD_DIM // KV_GROUPS
    gd = KV_GROUPS * HEAD_DIM
    n_pages = page_table.shape[1]
    past = n_pages * PAGE_SIZE
    scale = HEAD_DIM ** -0.5
    mp, ms = bp * tp, bs * ts
    kconv = w_dw_a.shape[1]
    kffn = w_dw_f.shape[1]

    xp = x_prompt.reshape(mp, d)
    xs = x_sample.reshape(ms, d)
    conv_p, conv_s, ffn_p, ffn_s = [], [], [], []

    def conv_ffn_both(xp, xs, layer):
        wu = w_up[layer].astype(BF16)
        wd = w_down[layer].astype(BF16)
        dw = w_dw_f[layer]
        hp = rmsnorm(xp, g_ffn[layer], BF16)
        act, sta, stb = ffn_up_prompt(hp, tp, wu, dw)
        ffn_p.append(jnp.concatenate([sta, stb], axis=-1)[:, STATE_ROWS - (kffn - 1):])
        xp = mm(act, wd, bm=512, bn=512, resid=xp, vmem_mib=60, name="ffn_down")
        hs = rmsnorm(xs, g_ffn[layer], BF16)
        up_s = mm(hs, wu, bm=ms, bn=512, name="ffn_up_sample").reshape(bs, ts, 2 * f)
        xx = jnp.concatenate([state_ffn_conv[layer], up_s], axis=1)
        ffn_s.append(xx[:, -(kffn - 1):])
        act_s = ffn_gate_sample(xx, ts, dw).reshape(ms, f).astype(BF16)
        xs = mm(act_s, wd, bm=ms, bn=512, resid=xs, vmem_mib=60, name="ffn_down_sample")
        return xp, xs

    for a in range(n_a):
        w1 = w_pw1[a].astype(BF16)
        w2 = w_pw2[a].astype(BF16)
        hp = rmsnorm(xp, g_attn[a], BF16)
        up = mm(hp, w1, bm=1024, bn=512, glu=True, name="pw1_glu")
        conv_p.append(up.reshape(bp, tp, d)[:, -(kconv - 1):])
        cp = conv_ln_prompt(up, tp, w_dw_a[a], ln_a_g[a], ln_a_b[a])
        xp = mm(cp, w2, bm=1024, bn=512, resid=xp, name="pw2")

        hs = rmsnorm(xs, g_attn[a], BF16)
        us = mm(hs, w1, bm=ms, bn=512, glu=True, name="pw1_glu_sample").reshape(bs, ts, d)
        xx = jnp.concatenate([state_conv_a[a], us], axis=1)
        conv_s.append(xx[:, -(kconv - 1):])
        cs = conv_ln_sample(xx, ts, w_dw_a[a], ln_a_g[a], ln_a_b[a]).reshape(ms, d).astype(BF16)
        xs = mm(cs, w2, bm=ms, bn=512, resid=xs, name="pw2_sample")

        xp, xs = conv_ffn_both(xp, xs, a)

    wkv = w_kv.astype(BF16)
    kv_p, kvb_p = mm(rmsnorm(xp, g_kv, BF16), wkv, bm=1024, bn=512, out_dtypes=(F32, BF16), name="kv")
    kv_s = mm(rmsnorm(xs, g_kv, BF16), wkv, bm=ms, bn=512, name="kv_sample")
    kv_p5 = kv_p.reshape(bp, tp, 2 * N_BRANCH, KV_GROUPS, HEAD_DIM)
    kv_s5 = kv_s.reshape(bs, ts, 2 * N_BRANCH, KV_GROUPS, HEAD_DIM)
    kv_s3 = kv_s.reshape(bs, ts, 2 * N_BRANCH * gd)

    half_w = CMP_STRIDE * gd
    nhp = tp // CMP_STRIDE
    n_cmp_p = nhp - 1
    assert tp % CMP_STRIDE == 0 and ts <= CMP_STRIDE and past % PAGE_SIZE == 0
    nhs = past // CMP_STRIDE
    n_cmp_s = nhs
    page_ids = page_table.reshape(-1)
    kcs = []
    for proj, (pe, w1c, w2c, pool) in enumerate([(cmp_pe_k, cmp_w1_k, cmp_w2_k, cache_cmp_k),
                                                 (cmp_pe_v, cmp_w1_v, cmp_w2_v, cache_cmp_v)]):
        pebig, wbig = _cmp_weights(pe, w1c)
        w2b = w2c.astype(BF16)
        lo, hi = cmp_proj(kv_p[:, proj * gd:(proj + 1) * gd].reshape(bp * nhp, half_w), pebig, wbig)
        lo = lo.reshape(bp, nhp, gd)
        hi = hi.reshape(bp, nhp, gd)
        hi_next = jnp.concatenate([hi[:, 1:], jnp.zeros((bp, 1, gd), F32)], axis=1)
        _, c_p = cmp_out(lo, hi_next, w2b)

        lo_h, hi_h = cmp_proj_paged(pool.reshape(pool.shape[0], PAGE_SIZE // CMP_STRIDE, half_w),
                                    page_ids, pebig, wbig)
        new_rows = jnp.pad(kv_s3[:, :, proj * gd:(proj + 1) * gd], ((0, 0), (0, CMP_STRIDE - ts), (0, 0)))
        _, hi_n = cmp_proj(new_rows.reshape(bs, half_w), pebig, wbig)
        hi_next_s = jnp.concatenate([hi_h.reshape(bs, nhs, gd)[:, 1:], hi_n[:, None]], axis=1)
        _, c_s = cmp_out(lo_h.reshape(bs, nhs, gd), hi_next_s, w2b)
        kcs.append((c_p, c_s))
    (kc_p, kc_s), (vc_p, vc_s) = kcs

    b_l = n_a
    wq = w_qg[0][:, :hd].astype(BF16)
    wg = _gate_weight(w_qg[0], hd, hpg)
    wo = w_o[0].astype(BF16)
    hp = rmsnorm(xp, g_attn[b_l], BF16)
    q_p = mm(hp, wq, bm=1024, bn=512, scale=scale, out_dtypes=(BF16,), name="q_proj")
    gate_p = mm(hp, wg, bm=1024, bn=512, act="sigmoid", name="gate_proj")
    o_p = nsa_prompt(q_p, gate_p, kc_p, vc_p, kvb_p, bp, tp, n_cmp_p)
    xp = mm(o_p, wo, bm=1024, bn=512, resid=xp, name="o_proj")

    hs = rmsnorm(xs, g_attn[b_l], BF16)
    q_s = mm(hs, wq, bm=ms, bn=512, scale=scale, out_dtypes=(BF16,), name="q_proj_sample")
    gate_s = mm(hs, wg, bm=ms, bn=512, act="sigmoid", name="gate_proj_sample")
    q_s3 = q_s.reshape(bs, ts, hd)
    n_sel_s = past // SEL_BLOCK + -(-ts // SEL_BLOCK)
    wbuf = cache_win_k.shape[1]
    o_cmp_s, o_win_s, sel_s = nsa_sample_cmp_win(
        q_s3, kc_s, vc_s, cache_win_k.reshape(bs, wbuf, gd), cache_win_v.reshape(bs, wbuf, gd),
        kv_s3, past, n_cmp_s, n_sel_s)
    o_s = nsa_sample_slc(q_s3, sel_s, kv_s3, gate_s.reshape(bs, ts, KV_GROUPS * LANES), o_cmp_s, o_win_s,
                         cache_slc_k.reshape(-1, PAGE_SIZE, gd), cache_slc_v.reshape(-1, PAGE_SIZE, gd),
                         page_table, past)
    xs = mm(o_s.reshape(ms, hd).astype(BF16), wo, bm=ms, bn=512, resid=xs, name="o_proj_sample")

    xp, xs = conv_ffn_both(xp, xs, b_l)

    y_prompt = rmsnorm(xp, g_final, F32).reshape(bp, tp, d)
    y_sample = rmsnorm(xs, g_final, F32).reshape(bs, ts, d)
    keep = min(WINDOW, tp)
    win_k_s = jnp.concatenate([cache_win_k, kv_s5[:, :, 4]], axis=1)[:, -wbuf:]
    win_v_s = jnp.concatenate([cache_win_v, kv_s5[:, :, 5]], axis=1)[:, -wbuf:]
    return (y_prompt, y_sample, jnp.stack(conv_p), jnp.stack(conv_s), jnp.stack(ffn_p), jnp.stack(ffn_s),
            kv_p5[:, :, 0], kv_p5[:, :, 1], kv_p5[:, :, 2], kv_p5[:, :, 3],
            kv_s5[:, :, 0], kv_s5[:, :, 1], kv_s5[:, :, 2], kv_s5[:, :, 3],
            kv_p5[:, -keep:, 4], kv_p5[:, -keep:, 5], win_k_s, win_v_s)
```

```python
import functools

import jax
import jax.numpy as jnp
from jax import lax
from jax.experimental import pallas as pl
from jax.experimental.pallas import tpu as pltpu

F32 = jnp.float32
BF16 = jnp.bfloat16

HEAD_DIM = 128
KV_GROUPS = 4
N_BRANCH = 3
CMP_BLOCK = 32
CMP_STRIDE = 16
SEL_BLOCK = 64
N_SELECT = 16
N_INIT_BLOCKS = 1
N_LOCAL_BLOCKS = 2
WINDOW = 512
PAGE_SIZE = 128
EPS = 1e-6
NEG = -1e30
FORCE = 1e9

LOG2E = 1.4426950408889634
LANES = 128
BF16_ROWS = 16
MIB = 1024 * 1024


def _cparams(sem, vmem_mib=48):
    return pltpu.CompilerParams(dimension_semantics=sem, vmem_limit_bytes=vmem_mib * MIB)


def _sds(shape, dtype):
    return jax.ShapeDtypeStruct(shape, dtype)


def _silu(x):
    return x * jax.nn.sigmoid(x)


def _rmsnorm_body(x_ref, g_ref, o_ref):
    x = x_ref[...]
    y = x * lax.rsqrt(jnp.mean(x * x, axis=-1, keepdims=True) + EPS)
    o_ref[...] = (y * g_ref[...]).astype(o_ref.dtype)


def rmsnorm(x, g, out_dtype, bm=512):
    m, d = x.shape
    bm = min(bm, m)
    return pl.pallas_call(
        _rmsnorm_body,
        out_shape=_sds((m, d), out_dtype),
        grid=(m // bm,),
        in_specs=[pl.BlockSpec((bm, d), lambda i: (i, 0)),
                  pl.BlockSpec((1, d), lambda i: (0, 0))],
        out_specs=pl.BlockSpec((bm, d), lambda i: (i, 0)),
        compiler_params=_cparams(("parallel",)),
        name="rmsnorm",
    )(x, g.reshape(1, d))


def _mm_body(*refs, glu, has_resid, scale, act, n_out):
    x_ref = refs[0]
    pos = 1
    x = x_ref[...]
    acc = jnp.dot(x, refs[pos][...], preferred_element_type=F32)
    pos += 1
    if glu:
        gate = jnp.dot(x, refs[pos][...], preferred_element_type=F32)
        pos += 1
        acc = acc * jax.nn.sigmoid(gate)
    if scale is not None:
        acc = acc * scale
    if act == "sigmoid":
        acc = jax.nn.sigmoid(acc)
    if has_resid:
        acc = acc + refs[pos][...]
        pos += 1
    for o_ref in refs[pos:pos + n_out]:
        o_ref[...] = acc.astype(o_ref.dtype)


def mm(x, w, *, bm, bn, glu=False, resid=None, scale=None, act=None,
       out_dtypes=(F32,), vmem_mib=48, name="mm"):
    m, k = x.shape
    n = w.shape[1] // 2 if glu else w.shape[1]
    bm = min(bm, m)
    bn = min(bn, n)
    assert m % bm == 0 and n % bn == 0
    in_specs = [pl.BlockSpec((bm, k), lambda i, j: (i, 0)),
                pl.BlockSpec((k, bn), lambda i, j: (0, j))]
    args = [x, w]
    if glu:
        nb = n // bn
        in_specs.append(pl.BlockSpec((k, bn), lambda i, j: (0, j + nb)))
        args.append(w)
    if resid is not None:
        in_specs.append(pl.BlockSpec((bm, bn), lambda i, j: (i, j)))
        args.append(resid)
    outs = pl.pallas_call(
        functools.partial(_mm_body, glu=glu, has_resid=resid is not None, scale=scale,
                          act=act, n_out=len(out_dtypes)),
        out_shape=[_sds((m, n), dt) for dt in out_dtypes],
        grid=(m // bm, n // bn),
        in_specs=in_specs,
        out_specs=[pl.BlockSpec((bm, bn), lambda i, j: (i, j)) for _ in out_dtypes],
        compiler_params=_cparams(("parallel", "arbitrary"), vmem_mib),
        name=name,
    )(*args)
    return outs[0] if len(out_dtypes) == 1 else outs


def _kv_proj_body(x_ref, w_ref, *o_refs, n_proj):
    j = pl.program_id(1)
    bm = x_ref.shape[0]
    acc = jnp.dot(x_ref[...], w_ref[...], preferred_element_type=F32)
    o_refs[n_proj][...] = acc.astype(BF16)
    for p in range(n_proj):
        @pl.when(j == p)
        def _(p=p):
            for g in range(KV_GROUPS):
                o_refs[p][pl.ds(g, bm, stride=KV_GROUPS), :] = acc[:, g * HEAD_DIM:(g + 1) * HEAD_DIM]


def kv_proj(x, w, bm=512):
    m, k = x.shape
    gd = KV_GROUPS * HEAD_DIM
    n_proj = w.shape[1] // gd
    assert m % bm == 0
    row_spec = pl.BlockSpec((bm * KV_GROUPS, HEAD_DIM), lambda i, j: (i, 0))
    outs = pl.pallas_call(
        functools.partial(_kv_proj_body, n_proj=n_proj),
        out_shape=[_sds((m * KV_GROUPS, HEAD_DIM), F32)] * n_proj + [_sds((m, n_proj * gd), BF16)],
        grid=(m // bm, n_proj),
        in_specs=[pl.BlockSpec((bm, k), lambda i, j: (i, 0)),
                  pl.BlockSpec((k, gd), lambda i, j: (0, j))],
        out_specs=[row_spec] * n_proj + [pl.BlockSpec((bm, gd), lambda i, j: (i, j))],
        compiler_params=_cparams(("parallel", "arbitrary")),
        name="kv_proj",
    )(x, w)
    return outs[:n_proj], outs[n_proj]


def _ln_silu(c, g, b):
    mu = jnp.mean(c, axis=-1, keepdims=True)
    xc = c - mu
    y = xc * lax.rsqrt(jnp.mean(xc * xc, axis=-1, keepdims=True) + EPS)
    return _silu(y * g + b)


CONV_HALO = 32


SUBLANES = 8


def _conv_ln_body(u_ref, halo_ref, w_ref, g_ref, b_ref, o_ref, xs_ref, sh_ref, c_ref,
                  *, bt, blocks_per_seq, kw, cw, rc):
    i = pl.program_id(0)
    d = u_ref.shape[1]
    keep = jnp.where(i % blocks_per_seq == 0, 0.0, 1.0)
    xs_ref[0:CONV_HALO, :] = halo_ref[...] * keep
    xs_ref[CONV_HALO:, :] = u_ref[...]
    base = CONV_HALO - (kw - 1)
    span = sh_ref.shape[1]

    def chunk(ci, carry):
        lanes = pl.ds(pl.multiple_of(ci * cw, cw), cw)
        for r in range(1, SUBLANES):
            sh_ref[r - 1] = xs_ref[r:r + span, lanes]
        for rb in range(bt // rc):
            acc = jnp.zeros((rc, cw), F32)
            for k in range(kw):
                a, r = divmod(base + k, SUBLANES)
                row0 = rb * rc + SUBLANES * a
                if r == 0:
                    x = xs_ref[row0:row0 + rc, lanes]
                else:
                    x = sh_ref[r - 1, row0:row0 + rc, :]
                acc = acc + w_ref[k:k + 1, lanes] * x
            c_ref[rb * rc:(rb + 1) * rc, lanes] = acc
        return carry

    lax.fori_loop(0, d // cw, chunk, 0)

    def ln(rb, carry):
        rows = pl.ds(pl.multiple_of(rb * rc, rc), rc)
        o_ref[rows, :] = _ln_silu(c_ref[rows, :], g_ref[...], b_ref[...]).astype(o_ref.dtype)
        return carry

    lax.fori_loop(0, bt // rc, ln, 0)


def conv_ln_prompt(u, seq, w_dw, ln_g, ln_b, bt=256, cw=512, rc=64):
    m, d = u.shape
    kw = w_dw.shape[0]
    assert kw - 1 <= CONV_HALO and seq % bt == 0 and bt % CONV_HALO == 0
    assert d % cw == 0 and bt % rc == 0
    r = bt // CONV_HALO
    span = bt + CONV_HALO - SUBLANES
    return pl.pallas_call(
        functools.partial(_conv_ln_body, bt=bt, blocks_per_seq=seq // bt, kw=kw, cw=cw, rc=rc),
        out_shape=_sds((m, d), BF16),
        grid=(m // bt,),
        in_specs=[pl.BlockSpec((bt, d), lambda i: (i, 0)),
                  pl.BlockSpec((CONV_HALO, d), lambda i: (jnp.maximum(i * r - 1, 0), 0)),
                  pl.BlockSpec((kw, d), lambda i: (0, 0)),
                  pl.BlockSpec((1, d), lambda i: (0, 0)),
                  pl.BlockSpec((1, d), lambda i: (0, 0))],
        out_specs=pl.BlockSpec((bt, d), lambda i: (i, 0)),
        scratch_shapes=[pltpu.VMEM((bt + CONV_HALO, d), F32),
                        pltpu.VMEM((SUBLANES - 1, span, cw), F32),
                        pltpu.VMEM((bt, d), F32)],
        compiler_params=_cparams(("parallel",)),
        name="conv_ln_prompt",
    )(u, u, w_dw, ln_g.reshape(1, d), ln_b.reshape(1, d))


def _conv_ln_sample_body(xx_ref, w_ref, g_ref, b_ref, o_ref, *, s, kw):
    acc = jnp.zeros((s, xx_ref.shape[-1]), F32)
    for k in range(kw):
        acc = acc + w_ref[k:k + 1, :] * xx_ref[k:k + s, :]
    o_ref[...] = _ln_silu(acc, g_ref[...], b_ref[...])


def conv_ln_sample(xx, s, w_dw, ln_g, ln_b):
    b, l, d = xx.shape
    kw = w_dw.shape[0]
    return pl.pallas_call(
        functools.partial(_conv_ln_sample_body, s=s, kw=kw),
        out_shape=_sds((b, s, d), F32),
        grid=(b,),
        in_specs=[pl.BlockSpec((None, l, d), lambda i: (i, 0, 0)),
                  pl.BlockSpec((kw, d), lambda i: (0, 0)),
                  pl.BlockSpec((1, d), lambda i: (0, 0)),
                  pl.BlockSpec((1, d), lambda i: (0, 0))],
        out_specs=pl.BlockSpec((None, s, d), lambda i: (i, 0, 0)),
        compiler_params=_cparams(("parallel",)),
        name="conv_ln_sample",
    )(xx, w_dw, ln_g.reshape(1, d), ln_b.reshape(1, d))


FFN_HALO = BF16_ROWS
STATE_ROWS = 8


def _ffn_up_body(h_ref, halo_ref, wa_ref, wb_ref, dwa_ref, dwb_ref, act_ref, sta_ref, stb_ref,
                 hx_ref, *, bm, blocks_per_seq, kw):
    i = pl.program_id(0)

    @pl.when(pl.program_id(1) == 0)
    def _():
        keep = jnp.where(i % blocks_per_seq == 0, 0.0, 1.0)
        hx_ref[0:FFN_HALO, :] = (halo_ref[...].astype(F32) * keep).astype(BF16)
        hx_ref[FFN_HALO:, :] = h_ref[...]

    hx = hx_ref[...]
    ua = jnp.dot(hx, wa_ref[...], preferred_element_type=F32)
    ub = jnp.dot(hx, wb_ref[...], preferred_element_type=F32)

    def conv(u, dw_ref):
        c = jnp.zeros((bm, u.shape[1]), F32)
        for k in range(kw):
            off = FFN_HALO - (kw - 1) + k
            c = c + dw_ref[k:k + 1, :] * u[off:off + bm, :]
        return c

    ca = conv(ua, dwa_ref)
    cb = conv(ub, dwb_ref)
    act_ref[...] = (_silu(ca) * cb).astype(act_ref.dtype)
    sta_ref[...] = ua[FFN_HALO + bm - STATE_ROWS:, :]
    stb_ref[...] = ub[FFN_HALO + bm - STATE_ROWS:, :]


def ffn_up_prompt(h, seq, w_up, w_dw, bm=1024, bn=256, vmem_mib=48):
    m, d = h.shape
    f = w_up.shape[1] // 2
    kw = w_dw.shape[0]
    nbatch = m // seq
    assert seq % bm == 0 and f % bn == 0 and kw - 1 <= FFN_HALO
    bps = seq // bm
    r = bm // FFN_HALO
    nb = f // bn
    act, sta, stb = pl.pallas_call(
        functools.partial(_ffn_up_body, bm=bm, blocks_per_seq=bps, kw=kw),
        out_shape=[_sds((m, f), BF16), _sds((m // bm, STATE_ROWS, f), F32),
                   _sds((m // bm, STATE_ROWS, f), F32)],
        grid=(m // bm, nb),
        in_specs=[pl.BlockSpec((bm, d), lambda i, j: (i, 0)),
                  pl.BlockSpec((FFN_HALO, d), lambda i, j: (jnp.maximum(i * r - 1, 0), 0)),
                  pl.BlockSpec((d, bn), lambda i, j: (0, j)),
                  pl.BlockSpec((d, bn), lambda i, j: (0, j + nb)),
                  pl.BlockSpec((kw, bn), lambda i, j: (0, j)),
                  pl.BlockSpec((kw, bn), lambda i, j: (0, j + nb))],
        out_specs=[pl.BlockSpec((bm, bn), lambda i, j: (i, j)),
                   pl.BlockSpec((None, STATE_ROWS, bn), lambda i, j: (i, 0, j)),
                   pl.BlockSpec((None, STATE_ROWS, bn), lambda i, j: (i, 0, j))],
        scratch_shapes=[pltpu.VMEM((bm + FFN_HALO, d), BF16)],
        compiler_params=_cparams(("parallel", "arbitrary"), vmem_mib),
        name="ffn_up_prompt",
    )(h, h, w_up, w_up, w_dw, w_dw)
    return act, sta[bps - 1::bps], stb[bps - 1::bps]


def _ffn_gate_sample_body(xa_ref, xb_ref, dwa_ref, dwb_ref, o_ref, *, s, kw):
    def conv(x_ref, dw_ref):
        c = jnp.zeros((x_ref.shape[0], s, x_ref.shape[2]), F32)
        for k in range(kw):
            c = c + dw_ref[k:k + 1, :] * x_ref[:, k:k + s, :]
        return c

    o_ref[...] = _silu(conv(xa_ref, dwa_ref)) * conv(xb_ref, dwb_ref)


def ffn_gate_sample(xx, s, w_dw):
    b, l, f2 = xx.shape
    f = f2 // 2
    kw = w_dw.shape[0]
    return pl.pallas_call(
        functools.partial(_ffn_gate_sample_body, s=s, kw=kw),
        out_shape=_sds((b, s, f), F32),
        grid=(1,),
        in_specs=[pl.BlockSpec((b, l, f), lambda i: (0, 0, 0)),
                  pl.BlockSpec((b, l, f), lambda i: (0, 0, 1)),
                  pl.BlockSpec((kw, f), lambda i: (0, 0)),
                  pl.BlockSpec((kw, f), lambda i: (0, 1))],
        out_specs=pl.BlockSpec((b, s, f), lambda i: (0, 0, 0)),
        compiler_params=_cparams(("arbitrary",)),
        name="ffn_gate_sample",
    )(xx, xx, w_dw, w_dw)


def _cmp_half_dots(xs_ref, pe_ref, w_ref, lo_ref, hi_ref):
    for g in range(KV_GROUPS):
        x = xs_ref[g]
        sl = slice(g * HEAD_DIM, (g + 1) * HEAD_DIM)
        lo_ref[:, sl] = jnp.dot((x + pe_ref[0:1, :]).astype(BF16), w_ref[0], preferred_element_type=F32)
        hi_ref[:, sl] = jnp.dot((x + pe_ref[1:2, :]).astype(BF16), w_ref[1], preferred_element_type=F32)


def cmp_proj_halves(xs, pe, w):
    _, r, kk = xs.shape
    n = w.shape[2]
    return pl.pallas_call(
        _cmp_half_dots,
        out_shape=[_sds((r, KV_GROUPS * n), F32)] * 2,
        grid=(1,),
        in_specs=[pl.BlockSpec((KV_GROUPS, r, kk), lambda i: (0, 0, 0)),
                  pl.BlockSpec((2, kk), lambda i: (0, 0)),
                  pl.BlockSpec((2, kk, n), lambda i: (0, 0, 0))],
        out_specs=[pl.BlockSpec((r, KV_GROUPS * n), lambda i: (0, 0))] * 2,
        compiler_params=_cparams(("arbitrary",)),
        name="cmp_proj_halves",
    )(xs, pe, w)


def _cmp_proj_pool_body(*refs, pps):
    pg_refs = refs[1:1 + pps]
    pe_ref, w_ref, lo_ref, hi_ref, xs_ref = refs[1 + pps:]
    hp = pg_refs[0].shape[0] // (CMP_STRIDE * KV_GROUPS)
    for p, pg_ref in enumerate(pg_refs):
        for j in range(CMP_STRIDE):
            for g in range(KV_GROUPS):
                rows = pl.ds(j * KV_GROUPS + g, hp, stride=CMP_STRIDE * KV_GROUPS)
                xs_ref[g, p * hp:(p + 1) * hp, j * HEAD_DIM:(j + 1) * HEAD_DIM] = pg_ref[rows, :]
    _cmp_half_dots(xs_ref, pe_ref, w_ref, lo_ref, hi_ref)


def cmp_proj_pool(pool, page_ids, pe, w, pages_per_step=32):
    _, pr, _ = pool.shape
    hp = pr // (CMP_STRIDE * KV_GROUPS)
    kk = CMP_STRIDE * HEAD_DIM
    n = w.shape[2]
    npg = page_ids.shape[0]
    pps = pages_per_step
    assert npg % pps == 0 and hp == SUBLANES
    bm = hp * pps

    def page_spec(p):
        return pl.BlockSpec((None, pr, HEAD_DIM), lambda i, ids: (ids[i * pps + p], 0, 0))

    grid_spec = pltpu.PrefetchScalarGridSpec(
        num_scalar_prefetch=1,
        grid=(npg // pps,),
        in_specs=[page_spec(p) for p in range(pps)] + [
            pl.BlockSpec((2, kk), lambda i, ids: (0, 0)),
            pl.BlockSpec((2, kk, n), lambda i, ids: (0, 0, 0))],
        out_specs=[pl.BlockSpec((bm, KV_GROUPS * n), lambda i, ids: (i, 0))] * 2,
        scratch_shapes=[pltpu.VMEM((KV_GROUPS, bm, kk), F32)],
    )
    return pl.pallas_call(
        functools.partial(_cmp_proj_pool_body, pps=pps),
        out_shape=[_sds((npg * hp, KV_GROUPS * n), F32)] * 2,
        grid_spec=grid_spec,
        compiler_params=_cparams(("arbitrary",)),
        name="cmp_proj_pool",
    )(page_ids, *([pool] * pps), pe, w)


def _cmp_out_body(lo_ref, hi_ref, w2_ref, o_ref, ob_ref):
    s = _silu(lo_ref[...] + hi_ref[...])
    for g in range(KV_GROUPS):
        sl = slice(g * HEAD_DIM, (g + 1) * HEAD_DIM)
        y = jnp.dot(s[:, sl].astype(BF16), w2_ref[...], preferred_element_type=F32)
        o_ref[:, sl] = y
        ob_ref[:, sl] = y.astype(BF16)


def cmp_out(lo, hi_next, w2, bn=128):
    b, nc, n = lo.shape
    bn = min(bn, nc)
    spec = pl.BlockSpec((None, bn, n), lambda i, j: (i, j, 0))
    return pl.pallas_call(
        _cmp_out_body,
        out_shape=[_sds((b, nc, n), F32), _sds((b, nc, n), BF16)],
        grid=(b, nc // bn),
        in_specs=[spec, spec, pl.BlockSpec((HEAD_DIM, HEAD_DIM), lambda i, j: (0, 0))],
        out_specs=[spec, spec],
        compiler_params=_cparams(("parallel", "parallel")),
        name="cmp_out",
    )(lo, hi_next, w2)


def _cmp_weights(pe, w1):
    half = CMP_STRIDE * HEAD_DIM
    return pe.reshape(2, half), w1.reshape(2, half, -1).astype(BF16)


def _iota(shape, dim):
    return lax.broadcasted_iota(jnp.int32, shape, dim)


def _dot_nt(a, b):
    return lax.dot_general(a, b, (((1,), (1,)), ((), ())), preferred_element_type=F32)


def _split3_dot(p, c):
    p1 = p.astype(BF16)
    r1 = p - p1.astype(F32)
    p2 = r1.astype(BF16)
    p3 = (r1 - p2.astype(F32)).astype(BF16)
    return (jnp.dot(p1, c, preferred_element_type=F32) + jnp.dot(p2, c, preferred_element_type=F32)
            + jnp.dot(p3, c, preferred_element_type=F32))


def _cover(n_cmp, n_sel):
    n = _iota((n_cmp, n_sel), 0)
    j = _iota((n_cmp, n_sel), 1)
    lo = jnp.maximum(n * CMP_STRIDE, j * SEL_BLOCK)
    hi = jnp.minimum(n * CMP_STRIDE + CMP_BLOCK, (j + 1) * SEL_BLOCK)
    return (jnp.maximum(hi - lo, 0).astype(F32) * (1.0 / CMP_STRIDE)).astype(BF16)


def _head_slopes(g, hpg, shape, dim):
    h = g * hpg + _iota(shape, dim) + 1
    return jnp.exp2(h.astype(F32) * (-8.0 / (KV_GROUPS * hpg)))


def _flash_step(s3, v, m_ref, l_ref, acc_ref):
    hh, tq, tk = s3.shape
    m_prev = m_ref[...]
    m_new = jnp.maximum(m_prev, jnp.max(s3, axis=-1, keepdims=True))
    alpha = jnp.exp(m_prev - m_new)
    p = jnp.exp(s3 - m_new)
    l_ref[...] = alpha * l_ref[...] + jnp.sum(p, axis=-1, keepdims=True)
    pv = jnp.dot(p.reshape(hh * tq, tk).astype(BF16), v, preferred_element_type=F32)
    acc_ref[...] = alpha * acc_ref[...] + pv.reshape(hh, tq, HEAD_DIM)
    m_ref[...] = m_new


def _flash_init(m_ref, l_ref, acc_ref):
    m_ref[...] = jnp.full(m_ref.shape, NEG, F32)
    l_ref[...] = jnp.zeros(l_ref.shape, F32)
    acc_ref[...] = jnp.zeros(acc_ref.shape, F32)


def _select_rank(score, idx, n_candidates):
    rank = jnp.zeros(score.shape, jnp.int32)
    for i in range(n_candidates):
        si = score[:, i:i + 1]
        before = (si > score) | ((si == score) & (i < idx))
        rank = rank + before.astype(jnp.int32)
    return rank


def _cover_t(n_sel, n_cmp):
    j = _iota((n_sel, n_cmp), 0)
    n = _iota((n_sel, n_cmp), 1)
    lo = jnp.maximum(n * CMP_STRIDE, j * SEL_BLOCK)
    hi = jnp.minimum(n * CMP_STRIDE + CMP_BLOCK, (j + 1) * SEL_BLOCK)
    return (jnp.maximum(hi - lo, 0).astype(F32) * (1.0 / CMP_STRIDE)).astype(BF16)


def _nsa_prompt_body(slopes_ref, q_ref, gate_ref, kc_ref, vc_ref, sk_ref, sv_ref, wk_ref, wv_ref, o_ref,
                     vts_ref, vtw_ref, vct_ref, sd_ref, qt_ref, selt_ref, m_ref, l_ref, acc_ref,
                     ocmp_ref, oslc_ref, *, tq, tk, seq, n_cmp, hpg):
    g = pl.program_id(1)
    i = pl.program_id(2)
    t0 = i * tq
    n_sel = seq // SEL_BLOCK
    n_pick = min(N_SELECT, n_sel)
    nselp = selt_ref.shape[0]
    ncp = kc_ref.shape[0]
    bpt = tk // SEL_BLOCK

    def slope(h):
        return slopes_ref[g * hpg + h]

    def cols(h):
        return slice(h * tq, (h + 1) * tq)

    d0 = _iota((tk, tq), 1) - _iota((tk, tq), 0)

    @pl.when(i == 0)
    def _():
        vts_ref[...] = sv_ref[...].astype(F32).T.astype(BF16)
        vtw_ref[...] = wv_ref[...].astype(F32).T.astype(BF16)
        vct_ref[...] = vc_ref[...].astype(F32).T.astype(BF16)
        d0f = d0.astype(F32)
        for h in range(hpg):
            sd_ref[h] = slope(h) * d0f

    for h in range(hpg):
        qt_ref[:, cols(h)] = q_ref[:, h * HEAD_DIM:(h + 1) * HEAD_DIM].astype(F32).T.astype(BF16)

    t_row = t0 + _iota((1, tq), 1)

    n_col = _iota((ncp, 1), 0)
    dist_c = t_row - (n_col * CMP_STRIDE + (CMP_BLOCK - 1))
    valid_c = (dist_c >= 0) & (n_col < n_cmp)
    dist_cf = dist_c.astype(F32)
    valid_cf = valid_c.astype(F32)
    s_all = jnp.dot(kc_ref[...], qt_ref[...], preferred_element_type=F32)
    psum = jnp.zeros((ncp, tq), F32)
    ps = []
    for h in range(hpg):
        s = jnp.where(valid_c, s_all[:, cols(h)] - slope(h) * dist_cf, NEG)
        e = jnp.exp2(s - jnp.max(s, axis=0, keepdims=True))
        p = e * (valid_cf * (1.0 / jnp.sum(e, axis=0, keepdims=True)))
        psum = psum + p
        ps.append(p.astype(BF16))
    ocmp_ref[...] = jnp.dot(vct_ref[...], jnp.concatenate(ps, axis=1), preferred_element_type=F32)

    cover_t = _cover_t(nselp, ncp)
    p1 = psum.astype(BF16)
    r1 = psum - p1.astype(F32)
    p2 = r1.astype(BF16)
    p3 = (r1 - p2.astype(F32)).astype(BF16)
    score = (jnp.dot(cover_t, p1, preferred_element_type=F32)
             + jnp.dot(cover_t, p2, preferred_element_type=F32)
             + jnp.dot(cover_t, p3, preferred_element_type=F32))
    j_col = _iota((nselp, 1), 0)
    cur = t_row // SEL_BLOCK
    forced = (j_col < N_INIT_BLOCKS) | ((cur - j_col >= 0) & (cur - j_col < N_LOCAL_BLOCKS))
    score = jnp.where(forced, FORCE, score)
    score = jnp.where(j_col * SEL_BLOCK <= t_row, score, NEG)
    score = jnp.where(j_col < n_sel, score, -jnp.inf)
    rank = jnp.zeros(score.shape, jnp.int32)
    for c in range(n_sel):
        sc = score[c:c + 1, :]
        rank = rank + ((sc > score) | ((sc == score) & (c < j_col))).astype(jnp.int32)
    selt_ref[...] = ((rank < n_pick) & (j_col < n_sel)).astype(F32)

    def init():
        m_ref[...] = jnp.full(m_ref.shape, NEG, F32)
        l_ref[...] = jnp.zeros(l_ref.shape, F32)
        acc_ref[...] = jnp.zeros(acc_ref.shape, F32)

    def flash_tile(k_tile, vt_tile, ok, shift):
        s_all = jnp.dot(k_tile, qt_ref[...], preferred_element_type=F32)
        ps, alphas = [], []
        for h in range(hpg):
            u = slope(h) * shift
            s = jnp.where(ok, s_all[:, cols(h)] - sd_ref[h], NEG)
            m_prev = m_ref[:, cols(h)]
            m_new = jnp.maximum(m_prev, jnp.max(s, axis=0, keepdims=True) - u)
            alpha = jnp.exp2(m_prev - m_new)
            p = jnp.exp2(s - (m_new + u))
            l_ref[:, cols(h)] = alpha * l_ref[:, cols(h)] + jnp.sum(p, axis=0, keepdims=True)
            m_ref[:, cols(h)] = m_new
            ps.append(p.astype(BF16))
            alphas.append(alpha)
        pv = jnp.dot(vt_tile, jnp.concatenate(ps, axis=1), preferred_element_type=F32)
        acc_ref[...] = jnp.concatenate(alphas, axis=1) * acc_ref[...] + pv

    init()

    def slc_step(kt, carry):
        k0 = pl.multiple_of(kt * tk, tk)
        picked = jnp.concatenate(
            [jnp.broadcast_to(selt_ref[pl.ds(kt * bpt + b, 1), :], (SEL_BLOCK, tq)) for b in range(bpt)],
            axis=0)
        ok = (picked > 0.5) & (d0 >= k0 - t0)
        flash_tile(sk_ref[pl.ds(k0, tk), :], vts_ref[:, pl.ds(k0, tk)], ok, (t0 - k0).astype(F32))
        return carry

    lax.fori_loop(0, (t0 + tq + tk - 1) // tk, slc_step, 0)
    oslc_ref[...] = acc_ref[...] * (1.0 / l_ref[...])

    init()

    def win_step(kt, carry):
        k0 = pl.multiple_of(kt * tk, tk)
        ok = (d0 >= k0 - t0) & (d0 < WINDOW + k0 - t0)
        flash_tile(wk_ref[pl.ds(k0, tk), :], vtw_ref[:, pl.ds(k0, tk)], ok, (t0 - k0).astype(F32))
        return carry

    lax.fori_loop(jnp.maximum(t0 - WINDOW + 1, 0) // tk, (t0 + tq + tk - 1) // tk, win_step, 0)
    inv_l = 1.0 / l_ref[...]

    gate_t = gate_ref[...].T
    for h in range(hpg):
        def grow(br):
            c = br * hpg + h
            return gate_t[c:c + 1, :]
        o_t = (grow(0) * ocmp_ref[:, cols(h)] + grow(1) * oslc_ref[:, cols(h)]
               + (grow(2) * inv_l[:, cols(h)]) * acc_ref[:, cols(h)])
        o_ref[:, h * HEAD_DIM:(h + 1) * HEAD_DIM] = o_t.T.astype(o_ref.dtype)


def nsa_prompt(q, gate, kc, vc, kvb, slopes, batch, seq, n_cmp, tq=256, tk=256):
    m, hd = q.shape
    hpg = hd // HEAD_DIM // KV_GROUPS
    gw = hpg * HEAD_DIM
    ncp = kc.shape[1]
    nq = seq // tq
    n_sel = seq // SEL_BLOCK
    nselp = -(-n_sel // SUBLANES) * SUBLANES
    assert seq % tq == 0 and seq % tk == 0 and tk % SEL_BLOCK == 0

    def kv_spec(proj):
        return pl.BlockSpec((seq, HEAD_DIM), lambda b, g, i: (b, proj * KV_GROUPS + g))

    return pl.pallas_call(
        functools.partial(_nsa_prompt_body, tq=tq, tk=tk, seq=seq, n_cmp=n_cmp, hpg=hpg),
        out_shape=_sds((m, hd), BF16),
        grid=(batch, KV_GROUPS, nq),
        in_specs=[pl.BlockSpec(memory_space=pltpu.SMEM),
                  pl.BlockSpec((tq, gw), lambda b, g, i: (b * nq + i, g)),
                  pl.BlockSpec((tq, LANES), lambda b, g, i: (b * nq + i, g)),
                  pl.BlockSpec((None, ncp, HEAD_DIM), lambda b, g, i: (b, 0, g)),
                  pl.BlockSpec((None, ncp, HEAD_DIM), lambda b, g, i: (b, 0, g)),
                  kv_spec(2), kv_spec(3), kv_spec(4), kv_spec(5)],
        out_specs=pl.BlockSpec((tq, gw), lambda b, g, i: (b * nq + i, g)),
        scratch_shapes=[pltpu.VMEM((HEAD_DIM, seq), BF16),
                        pltpu.VMEM((HEAD_DIM, seq), BF16),
                        pltpu.VMEM((HEAD_DIM, ncp), BF16),
                        pltpu.VMEM((hpg, tk, tq), F32),
                        pltpu.VMEM((HEAD_DIM, hpg * tq), BF16),
                        pltpu.VMEM((nselp, tq), F32),
                        pltpu.VMEM((1, hpg * tq), F32),
                        pltpu.VMEM((1, hpg * tq), F32),
                        pltpu.VMEM((HEAD_DIM, hpg * tq), F32),
                        pltpu.VMEM((HEAD_DIM, hpg * tq), F32),
                        pltpu.VMEM((HEAD_DIM, hpg * tq), F32)],
        compiler_params=_cparams(("parallel", "parallel", "arbitrary")),
        name="nsa_prompt",
    )(slopes, q, gate, kc, vc, kvb, kvb, kvb, kvb)


def _stack_heads(q, g, hpg):
    return jnp.concatenate(
        [q[:, (g * hpg + h) * HEAD_DIM:(g * hpg + h + 1) * HEAD_DIM] for h in range(hpg)],
        axis=0).astype(BF16)


def _nsa_sample_cmp_win_body(q_ref, kc_ref, vc_ref, cwk_ref, cwv_ref, nwk_ref, nwv_ref,
                             ocmp_ref, owin_ref, sel_ref, *, s, past, n_cmp, n_sel, hpg):
    q = q_ref[...].astype(F32)
    ncp = kc_ref.shape[0]
    nbp = sel_ref.shape[-1]
    wbuf = cwk_ref.shape[0]
    n_pick = min(N_SELECT, n_sel)
    pos = past + _iota((s, 1), 0)
    cover = _cover(ncp, nbp)
    n_row = _iota((1, ncp), 1)
    dist_c = pos - (n_row * CMP_STRIDE + (CMP_BLOCK - 1))
    valid_c = (dist_c >= 0) & (n_row < n_cmp)
    j_row = _iota((1, nbp), 1)
    cur = pos // SEL_BLOCK
    forced = (j_row < N_INIT_BLOCKS) | ((cur - j_row >= 0) & (cur - j_row < N_LOCAL_BLOCKS))
    pad_rows = LANES - s
    d_w1 = pos - (past - wbuf + _iota((1, wbuf), 1))
    ok_w1 = (d_w1 >= 0) & (d_w1 < WINDOW) & (past - wbuf + _iota((1, wbuf), 1) >= 0)
    d_w2 = pos - (past + _iota((1, LANES), 1))
    ok_w2 = (d_w2 >= 0) & (d_w2 < WINDOW) & (_iota((1, LANES), 1) < s)

    for g in range(KV_GROUPS):
        gs = slice(g * HEAD_DIM, (g + 1) * HEAD_DIM)
        qs = _stack_heads(q, g, hpg)
        slope3 = _head_slopes(g, hpg, (hpg, 1, 1), 0)

        s3 = _dot_nt(qs, kc_ref[:, gs]).reshape(hpg, s, ncp)
        s3 = jnp.where(valid_c[None], s3 - slope3 * dist_c.astype(F32)[None], NEG)
        e = jnp.exp(s3 - jnp.max(s3, axis=-1, keepdims=True))
        p3 = e / jnp.sum(e, axis=-1, keepdims=True) * valid_c.astype(F32)[None]
        ocmp_ref[g] = jnp.dot(p3.reshape(hpg * s, ncp).astype(BF16), vc_ref[:, gs],
                              preferred_element_type=F32)

        score = _split3_dot(jnp.sum(p3, axis=0), cover)
        score = jnp.where(forced, FORCE, score)
        score = jnp.where(j_row * SEL_BLOCK <= pos, score, NEG)
        score = jnp.where(j_row < n_sel, score, -jnp.inf)
        sel = (_select_rank(score, j_row, n_sel) < n_pick) & (j_row < n_sel)
        sel_ref[g] = sel.astype(F32)

        zpad = jnp.zeros((pad_rows, HEAD_DIM), F32)
        k2 = jnp.concatenate([nwk_ref[:, gs], zpad], axis=0).astype(BF16)
        v2 = jnp.concatenate([nwv_ref[:, gs], zpad], axis=0).astype(BF16)
        s1 = _dot_nt(qs, cwk_ref[:, gs].astype(BF16)).reshape(hpg, s, wbuf)
        s2 = _dot_nt(qs, k2).reshape(hpg, s, LANES)
        s1 = jnp.where(ok_w1[None], s1 - slope3 * d_w1.astype(F32)[None], NEG)
        s2 = jnp.where(ok_w2[None], s2 - slope3 * d_w2.astype(F32)[None], NEG)
        mx = jnp.maximum(jnp.max(s1, axis=-1, keepdims=True), jnp.max(s2, axis=-1, keepdims=True))
        p1 = jnp.exp(s1 - mx)
        p2 = jnp.exp(s2 - mx)
        den = jnp.sum(p1, axis=-1, keepdims=True) + jnp.sum(p2, axis=-1, keepdims=True)
        ow = (jnp.dot(p1.reshape(hpg * s, wbuf).astype(BF16), cwv_ref[:, gs].astype(BF16),
                      preferred_element_type=F32)
              + jnp.dot(p2.reshape(hpg * s, LANES).astype(BF16), v2, preferred_element_type=F32))
        owin_ref[g] = ow / den.reshape(hpg * s, 1)


def nsa_sample_cmp_win(q, kc, vc, cwk, cwv, kv_new, past, n_cmp, n_sel):
    b, s, hd = q.shape
    hpg = hd // HEAD_DIM // KV_GROUPS
    ncp = kc.shape[1]
    wbuf = cwk.shape[1]
    gd = KV_GROUPS * HEAD_DIM
    nbp = -(-n_sel // LANES) * LANES
    o_shape = _sds((b, KV_GROUPS, hpg * s, HEAD_DIM), F32)
    o_spec = pl.BlockSpec((None, KV_GROUPS, hpg * s, HEAD_DIM), lambda i: (i, 0, 0, 0))
    return pl.pallas_call(
        functools.partial(_nsa_sample_cmp_win_body, s=s, past=past, n_cmp=n_cmp, n_sel=n_sel, hpg=hpg),
        out_shape=[o_shape, o_shape, _sds((b, KV_GROUPS, s, nbp), F32)],
        grid=(b,),
        in_specs=[pl.BlockSpec((None, s, hd), lambda i: (i, 0, 0)),
                  pl.BlockSpec((None, ncp, gd), lambda i: (i, 0, 0)),
                  pl.BlockSpec((None, ncp, gd), lambda i: (i, 0, 0)),
                  pl.BlockSpec((None, wbuf, gd), lambda i: (i, 0, 0)),
                  pl.BlockSpec((None, wbuf, gd), lambda i: (i, 0, 0)),
                  pl.BlockSpec((None, s, gd), lambda i: (i, 0, 4)),
                  pl.BlockSpec((None, s, gd), lambda i: (i, 0, 5))],
        out_specs=[o_spec, o_spec,
                   pl.BlockSpec((None, KV_GROUPS, s, nbp), lambda i: (i, 0, 0, 0))],
        compiler_params=_cparams(("parallel",)),
        name="nsa_sample_cmp_win",
    )(q, kc, vc, cwk, cwv, kv_new, kv_new)


def _nsa_sample_slc_body(*refs, s, past, hpg, pps, n_chunks):
    ids_ref = refs[0]
    del ids_ref
    kp_refs = refs[1:1 + pps]
    vp_refs = refs[1 + pps:1 + 2 * pps]
    (q_ref, sel_ref, nk_ref, nv_ref, gate_ref, ocmp_ref, owin_ref, o_ref,
     qs_ref, m_ref, l_ref, acc_ref) = refs[1 + 2 * pps:]
    c = pl.program_id(1)
    nbp = sel_ref.shape[-1]
    tk = pps * PAGE_SIZE
    pos = past + _iota((s, 1), 0)

    @pl.when(c == 0)
    def _():
        q = q_ref[...].astype(F32)
        for g in range(KV_GROUPS):
            qs_ref[g] = _stack_heads(q, g, hpg)
            _flash_init(m_ref.at[g], l_ref.at[g], acc_ref.at[g])

    k0 = c * tk
    d = pos - (k0 + _iota((1, tk), 1))
    blk = (k0 + _iota((nbp, tk), 1)) // SEL_BLOCK
    expand = (_iota((nbp, tk), 0) == blk).astype(BF16)
    for g in range(KV_GROUPS):
        gs = slice(g * HEAD_DIM, (g + 1) * HEAD_DIM)
        slope3 = _head_slopes(g, hpg, (hpg, 1, 1), 0)
        grp = pl.ds(g, PAGE_SIZE, stride=KV_GROUPS)
        kk = jnp.concatenate([r[grp, :] for r in kp_refs], axis=0).astype(BF16)
        vv = jnp.concatenate([r[grp, :] for r in vp_refs], axis=0).astype(BF16)
        picked = jnp.dot(sel_ref[g].astype(BF16), expand, preferred_element_type=F32)
        ok = (picked > 0.5) & (d >= 0)
        sc = _dot_nt(qs_ref[g], kk).reshape(hpg, s, tk)
        sc = jnp.where(ok[None], sc - slope3 * d.astype(F32)[None], NEG)
        _flash_step(sc, vv, m_ref.at[g], l_ref.at[g], acc_ref.at[g])

    @pl.when(c == n_chunks - 1)
    def _():
        jn = past // SEL_BLOCK
        d2 = pos - (past + _iota((1, LANES), 1))
        in_new = (d2 >= 0) & (_iota((1, LANES), 1) < s)
        zpad = jnp.zeros((LANES - s, HEAD_DIM), F32)
        gate = gate_ref[...]
        for g in range(KV_GROUPS):
            gs = slice(g * HEAD_DIM, (g + 1) * HEAD_DIM)
            slope3 = _head_slopes(g, hpg, (hpg, 1, 1), 0)
            k2 = jnp.concatenate([nk_ref[:, gs], zpad], axis=0).astype(BF16)
            v2 = jnp.concatenate([nv_ref[:, gs], zpad], axis=0).astype(BF16)
            ok = in_new & (sel_ref[g][:, jn:jn + 1] > 0.5)
            sc = _dot_nt(qs_ref[g], k2).reshape(hpg, s, LANES)
            sc = jnp.where(ok[None], sc - slope3 * d2.astype(F32)[None], NEG)
            _flash_step(sc, v2, m_ref.at[g], l_ref.at[g], acc_ref.at[g])
            o_slc = acc_ref[g] / l_ref[g]
            o_cmp = ocmp_ref[g].reshape(hpg, s, HEAD_DIM)
            o_win = owin_ref[g].reshape(hpg, s, HEAD_DIM)
            for h in range(hpg):
                def gcol(br):
                    col = g * LANES + br * hpg + h
                    return gate[:, col:col + 1]
                o = gcol(0) * o_cmp[h] + gcol(1) * o_slc[h] + gcol(2) * o_win[h]
                hs = (g * hpg + h) * HEAD_DIM
                o_ref[:, hs:hs + HEAD_DIM] = o


def nsa_sample_slc(q, sel, kv_new, gate, o_cmp, o_win, pool_k, pool_v, page_table, past,
                   pages_per_step=4):
    b, s, hd = q.shape
    hpg = hd // HEAD_DIM // KV_GROUPS
    gd = KV_GROUPS * HEAD_DIM
    npages = page_table.shape[1]
    pps = pages_per_step
    assert npages % pps == 0 and s <= SEL_BLOCK and past % SEL_BLOCK == 0
    n_chunks = npages // pps
    nbp = sel.shape[-1]

    def page_spec(p):
        return pl.BlockSpec((None, PAGE_SIZE * KV_GROUPS, HEAD_DIM),
                            lambda i, c, ids: (ids[i * npages + c * pps + p], 0, 0))

    def per_b(shape):
        nd = len(shape)
        return pl.BlockSpec((None,) + shape, lambda i, c, ids: (i,) + (0,) * nd)

    o4 = (KV_GROUPS, hpg * s, HEAD_DIM)
    grid_spec = pltpu.PrefetchScalarGridSpec(
        num_scalar_prefetch=1,
        grid=(b, n_chunks),
        in_specs=[page_spec(p) for p in range(pps)] + [page_spec(p) for p in range(pps)] + [
            per_b((s, hd)), per_b((KV_GROUPS, s, nbp)),
            pl.BlockSpec((None, s, gd), lambda i, c, ids: (i, 0, 2)),
            pl.BlockSpec((None, s, gd), lambda i, c, ids: (i, 0, 3)),
            per_b((s, KV_GROUPS * LANES)), per_b(o4), per_b(o4)],
        out_specs=per_b((s, hd)),
        scratch_shapes=[pltpu.VMEM((KV_GROUPS, hpg * s, HEAD_DIM), BF16),
                        pltpu.VMEM((KV_GROUPS, hpg, s, 1), F32),
                        pltpu.VMEM((KV_GROUPS, hpg, s, 1), F32),
                        pltpu.VMEM((KV_GROUPS, hpg, s, HEAD_DIM), F32)],
    )
    return pl.pallas_call(
        functools.partial(_nsa_sample_slc_body, s=s, past=past, hpg=hpg, pps=pps, n_chunks=n_chunks),
        out_shape=_sds((b, s, hd), F32),
        grid_spec=grid_spec,
        compiler_params=_cparams(("parallel", "arbitrary")),
        name="nsa_sample_slc",
    )(page_table.reshape(-1), *([pool_k] * pps), *([pool_v] * pps),
      q, sel, kv_new, kv_new, gate, o_cmp, o_win)


def _gate_weight(w_qg, hd, hpg):
    wg = w_qg[:, hd:].reshape(-1, N_BRANCH, KV_GROUPS, hpg).transpose(0, 2, 1, 3)
    wg = wg.reshape(-1, KV_GROUPS, N_BRANCH * hpg)
    wg = jnp.pad(wg, ((0, 0), (0, 0), (0, LANES - N_BRANCH * hpg)))
    return wg.reshape(-1, KV_GROUPS * LANES).astype(BF16)


def kernel(x_prompt, x_sample, state_conv_a, state_ffn_conv, cache_cmp_k, cache_cmp_v, cache_slc_k,
           cache_slc_v, cache_win_k, cache_win_v, page_table, g_attn, g_ffn, g_final, w_pw1, w_dw_a,
           ln_a_g, ln_a_b, w_pw2, g_kv, w_kv, cmp_pe_k, cmp_w1_k, cmp_w2_k, cmp_pe_v, cmp_w1_v,
           cmp_w2_v, w_qg, w_o, w_up, w_dw_f, w_down):
    bp, tp, d = x_prompt.shape
    bs, ts, _ = x_sample.shape
    depth = g_attn.shape[0]
    n_a = w_pw1.shape[0]
    assert depth == n_a + 1, "one NSA layer after the convolution layers"
    f = w_down.shape[1]
    hd = w_o.shape[1]
    hpg = hd // HEAD_DIM // KV_GROUPS
    gd = KV_GROUPS * HEAD_DIM
    n_pages = page_table.shape[1]
    past = n_pages * PAGE_SIZE
    scale = HEAD_DIM ** -0.5
    mp, ms = bp * tp, bs * ts
    kconv = w_dw_a.shape[1]
    kffn = w_dw_f.shape[1]

    xp = x_prompt.reshape(mp, d)
    xs = x_sample.reshape(ms, d)
    conv_p, conv_s, ffn_p, ffn_s = [], [], [], []

    def conv_ffn_both(xp, xs, layer):
        wu = w_up[layer].astype(BF16)
        wd = w_down[layer].astype(BF16)
        dw = w_dw_f[layer]
        hp = rmsnorm(xp, g_ffn[layer], BF16)
        act, sta, stb = ffn_up_prompt(hp, tp, wu, dw)
        ffn_p.append(jnp.concatenate([sta, stb], axis=-1)[:, STATE_ROWS - (kffn - 1):])
        xp = mm(act, wd, bm=512, bn=512, resid=xp, vmem_mib=60, name="ffn_down")
        hs = rmsnorm(xs, g_ffn[layer], BF16)
        up_s = mm(hs, wu, bm=ms, bn=512, name="ffn_up_sample").reshape(bs, ts, 2 * f)
        xx = jnp.concatenate([state_ffn_conv[layer], up_s], axis=1)
        ffn_s.append(xx[:, -(kffn - 1):])
        act_s = ffn_gate_sample(xx, ts, dw).reshape(ms, f).astype(BF16)
        xs = mm(act_s, wd, bm=ms, bn=512, resid=xs, vmem_mib=60, name="ffn_down_sample")
        return xp, xs

    for a in range(n_a):
        w1 = w_pw1[a].astype(BF16)
        w2 = w_pw2[a].astype(BF16)
        hp = rmsnorm(xp, g_attn[a], BF16)
        up = mm(hp, w1, bm=1024, bn=512, glu=True, name="pw1_glu")
        conv_p.append(up.reshape(bp, tp, d)[:, -(kconv - 1):])
        cp = conv_ln_prompt(up, tp, w_dw_a[a], ln_a_g[a], ln_a_b[a])
        xp = mm(cp, w2, bm=1024, bn=512, resid=xp, name="pw2")

        hs = rmsnorm(xs, g_attn[a], BF16)
        us = mm(hs, w1, bm=ms, bn=512, glu=True, name="pw1_glu_sample").reshape(bs, ts, d)
        xx = jnp.concatenate([state_conv_a[a], us], axis=1)
        conv_s.append(xx[:, -(kconv - 1):])
        cs = conv_ln_sample(xx, ts, w_dw_a[a], ln_a_g[a], ln_a_b[a]).reshape(ms, d).astype(BF16)
        xs = mm(cs, w2, bm=ms, bn=512, resid=xs, name="pw2_sample")

        xp, xs = conv_ffn_both(xp, xs, a)

    wkv = w_kv.astype(BF16)
    kv_rows_p, kvb_p = kv_proj(rmsnorm(xp, g_kv, BF16), wkv)
    kv_p4 = [r.reshape(bp, tp, KV_GROUPS, HEAD_DIM) for r in kv_rows_p]
    kv_s = mm(rmsnorm(xs, g_kv, BF16), wkv, bm=ms, bn=512, name="kv_sample")
    kv_s5 = kv_s.reshape(bs, ts, 2 * N_BRANCH, KV_GROUPS, HEAD_DIM)
    kv_s3 = kv_s.reshape(bs, ts, 2 * N_BRANCH * gd)

    half_w = CMP_STRIDE * gd
    nhp = tp // CMP_STRIDE
    n_cmp_p = nhp - 1
    assert tp % CMP_STRIDE == 0 and ts <= CMP_STRIDE and past % PAGE_SIZE == 0
    nhs = past // CMP_STRIDE
    n_cmp_s = nhs
    page_ids = page_table.reshape(-1)
    kcs = []
    for proj, (pe, w1c, w2c, pool) in enumerate([(cmp_pe_k, cmp_w1_k, cmp_w2_k, cache_cmp_k),
                                                 (cmp_pe_v, cmp_w1_v, cmp_w2_v, cache_cmp_v)]):
        pebig, wbig = _cmp_weights(pe, w1c)
        w2b = w2c.astype(BF16)
        rows_per_page = PAGE_SIZE * KV_GROUPS
        lo, hi = cmp_proj_pool(kv_rows_p[proj].reshape(-1, rows_per_page, HEAD_DIM),
                               jnp.arange(mp // PAGE_SIZE, dtype=jnp.int32), pebig, wbig,
                               pages_per_step=tp // PAGE_SIZE)
        lo = lo.reshape(bp, nhp, gd)
        hi = hi.reshape(bp, nhp, gd)
        hi_next = jnp.concatenate([hi[:, 1:], jnp.zeros((bp, 1, gd), F32)], axis=1)
        _, c_p = cmp_out(lo, hi_next, w2b)

        lo_h, hi_h = cmp_proj_pool(pool.reshape(pool.shape[0], PAGE_SIZE * KV_GROUPS, HEAD_DIM),
                                   page_ids, pebig, wbig, pages_per_step=min(32, bs * n_pages))
        new_rows = jnp.pad(kv_s3[:, :, proj * gd:(proj + 1) * gd], ((0, 0), (0, CMP_STRIDE - ts), (0, 0)))
        new_halves = new_rows.reshape(bs, CMP_STRIDE, KV_GROUPS, HEAD_DIM).transpose(2, 0, 1, 3)
        _, hi_n = cmp_proj_halves(new_halves.reshape(KV_GROUPS, bs, CMP_STRIDE * HEAD_DIM), pebig, wbig)
        hi_next_s = jnp.concatenate([hi_h.reshape(bs, nhs, gd)[:, 1:], hi_n[:, None]], axis=1)
        _, c_s = cmp_out(lo_h.reshape(bs, nhs, gd), hi_next_s, w2b)
        kcs.append((c_p, c_s))
    (kc_p, kc_s), (vc_p, vc_s) = kcs

    b_l = n_a
    wq = w_qg[0][:, :hd].astype(BF16)
    wg = _gate_weight(w_qg[0], hd, hpg)
    wo = w_o[0].astype(BF16)
    hp = rmsnorm(xp, g_attn[b_l], BF16)
    q_p = mm(hp, wq, bm=1024, bn=512, scale=scale * LOG2E, out_dtypes=(BF16,), name="q_proj")
    gate_p = mm(hp, wg, bm=1024, bn=512, act="sigmoid", name="gate_proj")
    n_heads = hd // HEAD_DIM
    slopes = jnp.exp2(-8.0 * jnp.arange(1, n_heads + 1, dtype=F32) / n_heads) * LOG2E
    o_p = nsa_prompt(q_p, gate_p, kc_p, vc_p, kvb_p, slopes, bp, tp, n_cmp_p)
    xp = mm(o_p, wo, bm=1024, bn=512, resid=xp, name="o_proj")

    hs = rmsnorm(xs, g_attn[b_l], BF16)
    q_s = mm(hs, wq, bm=ms, bn=512, scale=scale, out_dtypes=(BF16,), name="q_proj_sample")
    gate_s = mm(hs, wg, bm=ms, bn=512, act="sigmoid", name="gate_proj_sample")
    q_s3 = q_s.reshape(bs, ts, hd)
    n_sel_s = past // SEL_BLOCK + -(-ts // SEL_BLOCK)
    wbuf = cache_win_k.shape[1]
    o_cmp_s, o_win_s, sel_s = nsa_sample_cmp_win(
        q_s3, kc_s, vc_s, cache_win_k.reshape(bs, wbuf, gd), cache_win_v.reshape(bs, wbuf, gd),
        kv_s3, past, n_cmp_s, n_sel_s)
    o_s = nsa_sample_slc(q_s3, sel_s, kv_s3, gate_s.reshape(bs, ts, KV_GROUPS * LANES), o_cmp_s, o_win_s,
                         cache_slc_k.reshape(-1, PAGE_SIZE * KV_GROUPS, HEAD_DIM),
                         cache_slc_v.reshape(-1, PAGE_SIZE * KV_GROUPS, HEAD_DIM),
                         page_table, past)
    xs = mm(o_s.reshape(ms, hd).astype(BF16), wo, bm=ms, bn=512, resid=xs, name="o_proj_sample")

    xp, xs = conv_ffn_both(xp, xs, b_l)

    y_prompt = rmsnorm(xp, g_final, F32).reshape(bp, tp, d)
    y_sample = rmsnorm(xs, g_final, F32).reshape(bs, ts, d)
    keep = min(WINDOW, tp)
    win_k_s = jnp.concatenate([cache_win_k, kv_s5[:, :, 4]], axis=1)[:, -wbuf:]
    win_v_s = jnp.concatenate([cache_win_v, kv_s5[:, :, 5]], axis=1)[:, -wbuf:]
    return (y_prompt, y_sample, jnp.stack(conv_p), jnp.stack(conv_s), jnp.stack(ffn_p), jnp.stack(ffn_s),
            kv_p4[0], kv_p4[1], kv_p4[2], kv_p4[3],
            kv_s5[:, :, 0], kv_s5[:, :, 1], kv_s5[:, :, 2], kv_s5[:, :, 3],
            kv_p4[4][:, -keep:], kv_p4[5][:, -keep:], win_k_s, win_v_s)
```

```python
import functools

import jax
import jax.numpy as jnp
from jax import lax
from jax.experimental import pallas as pl
from jax.experimental.pallas import tpu as pltpu

F32 = jnp.float32
BF16 = jnp.bfloat16

HEAD_DIM = 128
KV_GROUPS = 4
N_BRANCH = 3
CMP_BLOCK = 32
CMP_STRIDE = 16
SEL_BLOCK = 64
N_SELECT = 16
N_INIT_BLOCKS = 1
N_LOCAL_BLOCKS = 2
WINDOW = 512
PAGE_SIZE = 128
EPS = 1e-6
NEG = -1e30
FORCE = 1e9

LOG2E = 1.4426950408889634
LANES = 128
BF16_ROWS = 16
MIB = 1024 * 1024


def _cparams(sem, vmem_mib=48):
    return pltpu.CompilerParams(dimension_semantics=sem, vmem_limit_bytes=vmem_mib * MIB)


def _sds(shape, dtype):
    return jax.ShapeDtypeStruct(shape, dtype)


def _silu(x):
    return x * jax.nn.sigmoid(x)


def _rmsnorm_body(x_ref, g_ref, o_ref):
    x = x_ref[...]
    y = x * lax.rsqrt(jnp.mean(x * x, axis=-1, keepdims=True) + EPS)
    o_ref[...] = (y * g_ref[...]).astype(o_ref.dtype)


def rmsnorm(x, g, out_dtype, bm=512):
    m, d = x.shape
    bm = min(bm, m)
    return pl.pallas_call(
        _rmsnorm_body,
        out_shape=_sds((m, d), out_dtype),
        grid=(m // bm,),
        in_specs=[pl.BlockSpec((bm, d), lambda i: (i, 0)),
                  pl.BlockSpec((1, d), lambda i: (0, 0))],
        out_specs=pl.BlockSpec((bm, d), lambda i: (i, 0)),
        compiler_params=_cparams(("parallel",)),
        name="rmsnorm",
    )(x, g.reshape(1, d))


def _mm_body(*refs, glu, has_resid, scale, act, n_out):
    x_ref = refs[0]
    pos = 1
    x = x_ref[...]
    acc = jnp.dot(x, refs[pos][...], preferred_element_type=F32)
    pos += 1
    if glu:
        gate = jnp.dot(x, refs[pos][...], preferred_element_type=F32)
        pos += 1
        acc = acc * jax.nn.sigmoid(gate)
    if scale is not None:
        acc = acc * scale
    if act == "sigmoid":
        acc = jax.nn.sigmoid(acc)
    if has_resid:
        acc = acc + refs[pos][...]
        pos += 1
    for o_ref in refs[pos:pos + n_out]:
        o_ref[...] = acc.astype(o_ref.dtype)


def mm(x, w, *, bm, bn, layer=None, glu=False, resid=None, scale=None, act=None,
       out_dtypes=(F32,), vmem_mib=48, name="mm"):
    m, k = x.shape
    n = w.shape[-1] // 2 if glu else w.shape[-1]
    bm = min(bm, m)
    bn = min(bn, n)
    assert m % bm == 0 and n % bn == 0
    nb = n // bn
    if w.ndim == 3:
        def w_spec(off):
            return pl.BlockSpec((None, k, bn), lambda i, j: (layer, 0, j + off))
    else:
        def w_spec(off):
            return pl.BlockSpec((k, bn), lambda i, j: (0, j + off))
    in_specs = [pl.BlockSpec((bm, k), lambda i, j: (i, 0)), w_spec(0)]
    args = [x, w]
    if glu:
        in_specs.append(w_spec(nb))
        args.append(w)
    if resid is not None:
        in_specs.append(pl.BlockSpec((bm, bn), lambda i, j: (i, j)))
        args.append(resid)
    outs = pl.pallas_call(
        functools.partial(_mm_body, glu=glu, has_resid=resid is not None, scale=scale,
                          act=act, n_out=len(out_dtypes)),
        out_shape=[_sds((m, n), dt) for dt in out_dtypes],
        grid=(m // bm, n // bn),
        in_specs=in_specs,
        out_specs=[pl.BlockSpec((bm, bn), lambda i, j: (i, j)) for _ in out_dtypes],
        compiler_params=_cparams(("parallel", "arbitrary"), vmem_mib),
        name=name,
    )(*args)
    return outs[0] if len(out_dtypes) == 1 else outs


def _mm2_body(*refs, glu, has_resid, scale, act, ni):
    it = iter(refs)
    xp_ref, xs_ref, wa_ref = next(it), next(it), next(it)
    wb_ref = next(it) if glu else None
    rp_ref, rs_ref = (next(it), next(it)) if has_resid else (None, None)
    yp_ref, ys_ref, wsa_ref = next(it), next(it), next(it)
    wsb_ref = next(it) if glu else None
    i = pl.program_id(1)

    @pl.when(i == 0)
    def _():
        wsa_ref[...] = wa_ref[...].astype(BF16)
        if glu:
            wsb_ref[...] = wb_ref[...].astype(BF16)

    def compute(x_ref, r_ref, y_ref):
        x = x_ref[...]
        acc = jnp.dot(x, wsa_ref[...], preferred_element_type=F32)
        if glu:
            acc = acc * jax.nn.sigmoid(jnp.dot(x, wsb_ref[...], preferred_element_type=F32))
        if scale is not None:
            acc = acc * scale
        if act == "sigmoid":
            acc = jax.nn.sigmoid(acc)
        if r_ref is not None:
            acc = acc + r_ref[...]
        y_ref[...] = acc.astype(y_ref.dtype)

    @pl.when(i < ni)
    def _():
        compute(xp_ref, rp_ref, yp_ref)

    @pl.when(i == ni)
    def _():
        compute(xs_ref, rs_ref, ys_ref)


def mm2(xp, xs, w, *, n=None, layer=None, bm=1024, bn=512, glu=False, resid=None, scale=None, act=None,
        out_dtype=F32, vmem_mib=56, name="mm2"):
    mp, k = xp.shape
    ms = xs.shape[0]
    nw = w.shape[-1]
    if n is None:
        n = nw // 2 if glu else nw
    bm = min(bm, mp)
    bn = min(bn, n)
    assert mp % bm == 0 and n % bn == 0
    ni, nb = mp // bm, n // bn
    if w.ndim == 3:
        def w_spec(off):
            return pl.BlockSpec((None, k, bn), lambda j, i: (layer, 0, j + off))
    else:
        def w_spec(off):
            return pl.BlockSpec((k, bn), lambda j, i: (0, j + off))
    prow = lambda j, i: (jnp.minimum(i, ni - 1), 0)
    pblk = lambda j, i: (jnp.minimum(i, ni - 1), j)
    in_specs = [pl.BlockSpec((bm, k), prow), pl.BlockSpec((ms, k), lambda j, i: (0, 0)), w_spec(0)]
    args = [xp, xs, w]
    if glu:
        in_specs.append(w_spec(nb))
        args.append(w)
    if resid is not None:
        in_specs += [pl.BlockSpec((bm, bn), pblk), pl.BlockSpec((ms, bn), lambda j, i: (0, j))]
        args += list(resid)
    scratch = [pltpu.VMEM((k, bn), BF16)] * (2 if glu else 1)
    return pl.pallas_call(
        functools.partial(_mm2_body, glu=glu, has_resid=resid is not None, scale=scale, act=act, ni=ni),
        out_shape=[_sds((mp, n), out_dtype), _sds((ms, n), out_dtype)],
        grid=(nb, ni + 1),
        in_specs=in_specs,
        out_specs=[pl.BlockSpec((bm, bn), pblk), pl.BlockSpec((ms, bn), lambda j, i: (0, j))],
        scratch_shapes=scratch,
        compiler_params=_cparams(("parallel", "arbitrary"), vmem_mib),
        name=name,
    )(*args)


def _kv_proj_body(x_ref, w_ref, *o_refs, n_proj):
    j = pl.program_id(1)
    bm = x_ref.shape[0]
    acc = jnp.dot(x_ref[...], w_ref[...], preferred_element_type=F32)
    o_refs[n_proj][...] = acc.astype(BF16)
    for p in range(n_proj):
        @pl.when(j == p)
        def _(p=p):
            for g in range(KV_GROUPS):
                o_refs[p][pl.ds(g, bm, stride=KV_GROUPS), :] = acc[:, g * HEAD_DIM:(g + 1) * HEAD_DIM]


def kv_proj(x, w, bm=512):
    m, k = x.shape
    gd = KV_GROUPS * HEAD_DIM
    n_proj = w.shape[1] // gd
    assert m % bm == 0
    row_spec = pl.BlockSpec((bm * KV_GROUPS, HEAD_DIM), lambda i, j: (i, 0))
    outs = pl.pallas_call(
        functools.partial(_kv_proj_body, n_proj=n_proj),
        out_shape=[_sds((m * KV_GROUPS, HEAD_DIM), F32)] * n_proj + [_sds((m, n_proj * gd), BF16)],
        grid=(m // bm, n_proj),
        in_specs=[pl.BlockSpec((bm, k), lambda i, j: (i, 0)),
                  pl.BlockSpec((k, gd), lambda i, j: (0, j))],
        out_specs=[row_spec] * n_proj + [pl.BlockSpec((bm, gd), lambda i, j: (i, j))],
        compiler_params=_cparams(("parallel", "arbitrary")),
        name="kv_proj",
    )(x, w)
    return outs[:n_proj], outs[n_proj]


def _ln_silu(c, g, b):
    mu = jnp.mean(c, axis=-1, keepdims=True)
    xc = c - mu
    y = xc * lax.rsqrt(jnp.mean(xc * xc, axis=-1, keepdims=True) + EPS)
    return _silu(y * g + b)


CONV_HALO = 32


SUBLANES = 8


def _conv_ln_body(u_ref, halo_ref, w_ref, g_ref, b_ref, o_ref, xs_ref, sh_ref, c_ref,
                  *, bt, blocks_per_seq, kw, cw, rc):
    i = pl.program_id(0)
    d = u_ref.shape[1]
    keep = jnp.where(i % blocks_per_seq == 0, 0.0, 1.0)
    xs_ref[0:CONV_HALO, :] = halo_ref[...] * keep
    xs_ref[CONV_HALO:, :] = u_ref[...]
    base = CONV_HALO - (kw - 1)
    span = sh_ref.shape[1]

    def chunk(ci, carry):
        lanes = pl.ds(pl.multiple_of(ci * cw, cw), cw)
        for r in range(1, SUBLANES):
            sh_ref[r - 1] = xs_ref[r:r + span, lanes]
        for rb in range(bt // rc):
            acc = jnp.zeros((rc, cw), F32)
            for k in range(kw):
                a, r = divmod(base + k, SUBLANES)
                row0 = rb * rc + SUBLANES * a
                if r == 0:
                    x = xs_ref[row0:row0 + rc, lanes]
                else:
                    x = sh_ref[r - 1, row0:row0 + rc, :]
                acc = acc + w_ref[k:k + 1, lanes] * x
            c_ref[rb * rc:(rb + 1) * rc, lanes] = acc
        return carry

    lax.fori_loop(0, d // cw, chunk, 0)

    def ln(rb, carry):
        rows = pl.ds(pl.multiple_of(rb * rc, rc), rc)
        o_ref[rows, :] = _ln_silu(c_ref[rows, :], g_ref[...], b_ref[...]).astype(o_ref.dtype)
        return carry

    lax.fori_loop(0, bt // rc, ln, 0)


def conv_ln_prompt(u, seq, w_dw, ln_g, ln_b, bt=256, cw=512, rc=64):
    m, d = u.shape
    kw = w_dw.shape[0]
    assert kw - 1 <= CONV_HALO and seq % bt == 0 and bt % CONV_HALO == 0
    assert d % cw == 0 and bt % rc == 0
    r = bt // CONV_HALO
    span = bt + CONV_HALO - SUBLANES
    return pl.pallas_call(
        functools.partial(_conv_ln_body, bt=bt, blocks_per_seq=seq // bt, kw=kw, cw=cw, rc=rc),
        out_shape=_sds((m, d), BF16),
        grid=(m // bt,),
        in_specs=[pl.BlockSpec((bt, d), lambda i: (i, 0)),
                  pl.BlockSpec((CONV_HALO, d), lambda i: (jnp.maximum(i * r - 1, 0), 0)),
                  pl.BlockSpec((kw, d), lambda i: (0, 0)),
                  pl.BlockSpec((1, d), lambda i: (0, 0)),
                  pl.BlockSpec((1, d), lambda i: (0, 0))],
        out_specs=pl.BlockSpec((bt, d), lambda i: (i, 0)),
        scratch_shapes=[pltpu.VMEM((bt + CONV_HALO, d), F32),
                        pltpu.VMEM((SUBLANES - 1, span, cw), F32),
                        pltpu.VMEM((bt, d), F32)],
        compiler_params=_cparams(("parallel",)),
        name="conv_ln_prompt",
    )(u, u, w_dw, ln_g.reshape(1, d), ln_b.reshape(1, d))


def _conv_ln_sample_body(xx_ref, w_ref, g_ref, b_ref, o_ref, *, s, kw):
    acc = jnp.zeros((s, xx_ref.shape[-1]), F32)
    for k in range(kw):
        acc = acc + w_ref[k:k + 1, :] * xx_ref[k:k + s, :]
    o_ref[...] = _ln_silu(acc, g_ref[...], b_ref[...])


def conv_ln_sample(xx, s, w_dw, ln_g, ln_b):
    b, l, d = xx.shape
    kw = w_dw.shape[0]
    return pl.pallas_call(
        functools.partial(_conv_ln_sample_body, s=s, kw=kw),
        out_shape=_sds((b, s, d), F32),
        grid=(b,),
        in_specs=[pl.BlockSpec((None, l, d), lambda i: (i, 0, 0)),
                  pl.BlockSpec((kw, d), lambda i: (0, 0)),
                  pl.BlockSpec((1, d), lambda i: (0, 0)),
                  pl.BlockSpec((1, d), lambda i: (0, 0))],
        out_specs=pl.BlockSpec((None, s, d), lambda i: (i, 0, 0)),
        compiler_params=_cparams(("parallel",)),
        name="conv_ln_sample",
    )(xx, w_dw, ln_g.reshape(1, d), ln_b.reshape(1, d))


FFN_HALO = BF16_ROWS
STATE_ROWS = 8


def _ffn_up_body(h_ref, hs_ref, wa_ref, wb_ref, dwa_ref, dwb_ref,
                 act_ref, sta_ref, stb_ref, usa_ref, usb_ref,
                 wsa_ref, wsb_ref, ua0_ref, ua1_ref, ub0_ref, ub1_ref,
                 *, bm, ni, blocks_per_seq, kw, mc, ec):
    i = pl.program_id(1)
    ua = (ua0_ref, ua1_ref)
    ub = (ub0_ref, ub1_ref)
    hist = slice(0, STATE_ROWS)

    def dots(slot, r0):
        h = h_ref[r0:r0 + mc, :]
        rows = slice(STATE_ROWS + r0, STATE_ROWS + r0 + mc)
        ua[slot][rows, :] = jnp.dot(h, wsa_ref[...], preferred_element_type=F32)
        ub[slot][rows, :] = jnp.dot(h, wsb_ref[...], preferred_element_type=F32)

    def conv(u_ref, dw_ref, r0):
        c = jnp.zeros((ec, u_ref.shape[1]), F32)
        for k in range(kw):
            off = STATE_ROWS - (kw - 1) + k + r0
            c = c + dw_ref[k:k + 1, :] * u_ref[off:off + ec, :]
        return c

    def gate(prev, r0):
        act_ref[r0:r0 + ec, :] = (_silu(conv(ua[prev], dwa_ref, r0))
                                  * conv(ub[prev], dwb_ref, r0)).astype(act_ref.dtype)

    def finish(prev):
        last_a = ua[prev][bm:bm + STATE_ROWS, :]
        last_b = ub[prev][bm:bm + STATE_ROWS, :]
        sta_ref[...] = last_a
        stb_ref[...] = last_b
        keep = jnp.where(i % blocks_per_seq == 0, 0.0, 1.0)
        ua[1 - prev][hist, :] = last_a * keep
        ub[1 - prev][hist, :] = last_b * keep

    @pl.when(i == 0)
    def _():
        wsa_ref[...] = wa_ref[...].astype(BF16)
        wsb_ref[...] = wb_ref[...].astype(BF16)
        ua0_ref[hist, :] = jnp.zeros((STATE_ROWS, ua0_ref.shape[1]), F32)
        ub0_ref[hist, :] = jnp.zeros((STATE_ROWS, ub0_ref.shape[1]), F32)
        for r0 in range(0, bm, mc):
            dots(0, r0)

    for parity in (0, 1):
        @pl.when((i >= 1) & (i < ni) & (i % 2 == parity))
        def _(parity=parity):
            for r0 in range(0, bm, mc):
                dots(parity, r0)
                for e0 in range(r0, r0 + mc, ec):
                    gate(1 - parity, e0)
            finish(1 - parity)

    @pl.when(i == ni)
    def _():
        for e0 in range(0, bm, ec):
            gate((ni - 1) % 2, e0)
        finish((ni - 1) % 2)
        hs = hs_ref[...]
        usa_ref[...] = jnp.dot(hs, wsa_ref[...], preferred_element_type=F32)
        usb_ref[...] = jnp.dot(hs, wsb_ref[...], preferred_element_type=F32)


def ffn_up(hp, hs, seq, w_up, layer, w_dw, bm=1024, bn=256, vmem_mib=56):
    m, d = hp.shape
    ms = hs.shape[0]
    f = w_up.shape[-1] // 2
    kw = w_dw.shape[1]
    assert seq % bm == 0 and f % bn == 0 and kw - 1 <= STATE_ROWS
    bps = seq // bm
    ni = m // bm
    nb = f // bn
    prev_blk = lambda j, i: (jnp.maximum(i - 1, 0), j)
    prev_state = lambda j, i: (jnp.maximum(i - 1, 0), 0, j)
    u_buf = pltpu.VMEM((STATE_ROWS + bm, bn), F32)
    act, sta, stb, usa, usb = pl.pallas_call(
        functools.partial(_ffn_up_body, bm=bm, ni=ni, blocks_per_seq=bps, kw=kw,
                          mc=min(256, bm), ec=min(128, bm)),
        out_shape=[_sds((m, f), BF16), _sds((ni, STATE_ROWS, f), F32), _sds((ni, STATE_ROWS, f), F32),
                   _sds((ms, f), F32), _sds((ms, f), F32)],
        grid=(nb, ni + 1),
        in_specs=[pl.BlockSpec((bm, d), lambda j, i: (jnp.minimum(i, ni - 1), 0)),
                  pl.BlockSpec((ms, d), lambda j, i: (0, 0)),
                  pl.BlockSpec((None, d, bn), lambda j, i: (layer, 0, j)),
                  pl.BlockSpec((None, d, bn), lambda j, i: (layer, 0, j + nb)),
                  pl.BlockSpec((None, kw, bn), lambda j, i: (layer, 0, j)),
                  pl.BlockSpec((None, kw, bn), lambda j, i: (layer, 0, j + nb))],
        out_specs=[pl.BlockSpec((bm, bn), prev_blk),
                   pl.BlockSpec((None, STATE_ROWS, bn), prev_state),
                   pl.BlockSpec((None, STATE_ROWS, bn), prev_state),
                   pl.BlockSpec((ms, bn), lambda j, i: (0, j)),
                   pl.BlockSpec((ms, bn), lambda j, i: (0, j))],
        scratch_shapes=[pltpu.VMEM((d, bn), BF16), pltpu.VMEM((d, bn), BF16), u_buf, u_buf, u_buf, u_buf],
        compiler_params=_cparams(("parallel", "arbitrary"), vmem_mib),
        name="ffn_up",
    )(hp, hs, w_up, w_up, w_dw, w_dw)
    return act, sta[bps - 1::bps], stb[bps - 1::bps], usa, usb


def _ffn_gate_sample_body(xa_ref, xb_ref, dwa_ref, dwb_ref, o_ref, *, s, kw):
    def conv(x_ref, dw_ref):
        c = jnp.zeros((x_ref.shape[0], s, x_ref.shape[2]), F32)
        for k in range(kw):
            c = c + dw_ref[k:k + 1, :] * x_ref[:, k:k + s, :]
        return c

    o_ref[...] = _silu(conv(xa_ref, dwa_ref)) * conv(xb_ref, dwb_ref)


def ffn_gate_sample(xx, s, w_dw):
    b, l, f2 = xx.shape
    f = f2 // 2
    kw = w_dw.shape[0]
    return pl.pallas_call(
        functools.partial(_ffn_gate_sample_body, s=s, kw=kw),
        out_shape=_sds((b, s, f), F32),
        grid=(1,),
        in_specs=[pl.BlockSpec((b, l, f), lambda i: (0, 0, 0)),
                  pl.BlockSpec((b, l, f), lambda i: (0, 0, 1)),
                  pl.BlockSpec((kw, f), lambda i: (0, 0)),
                  pl.BlockSpec((kw, f), lambda i: (0, 1))],
        out_specs=pl.BlockSpec((b, s, f), lambda i: (0, 0, 0)),
        compiler_params=_cparams(("arbitrary",)),
        name="ffn_gate_sample",
    )(xx, xx, w_dw, w_dw)


def _cmp_half_dots(xs_ref, pe_ref, w_ref, lo_ref, hi_ref):
    for g in range(KV_GROUPS):
        x = xs_ref[g]
        sl = slice(g * HEAD_DIM, (g + 1) * HEAD_DIM)
        lo_ref[:, sl] = jnp.dot((x + pe_ref[0:1, :]).astype(BF16), w_ref[0], preferred_element_type=F32)
        hi_ref[:, sl] = jnp.dot((x + pe_ref[1:2, :]).astype(BF16), w_ref[1], preferred_element_type=F32)


def cmp_proj_halves(xs, pe, w):
    _, r, kk = xs.shape
    n = w.shape[2]
    return pl.pallas_call(
        _cmp_half_dots,
        out_shape=[_sds((r, KV_GROUPS * n), F32)] * 2,
        grid=(1,),
        in_specs=[pl.BlockSpec((KV_GROUPS, r, kk), lambda i: (0, 0, 0)),
                  pl.BlockSpec((2, kk), lambda i: (0, 0)),
                  pl.BlockSpec((2, kk, n), lambda i: (0, 0, 0))],
        out_specs=[pl.BlockSpec((r, KV_GROUPS * n), lambda i: (0, 0))] * 2,
        compiler_params=_cparams(("arbitrary",)),
        name="cmp_proj_halves",
    )(xs, pe, w)


def _cmp_proj_pool_body(*refs, pps):
    pg_refs = refs[1:1 + pps]
    pe_ref, w_ref, lo_ref, hi_ref, xs_ref = refs[1 + pps:]
    hp = pg_refs[0].shape[0] // (CMP_STRIDE * KV_GROUPS)
    for p, pg_ref in enumerate(pg_refs):
        for j in range(CMP_STRIDE):
            for g in range(KV_GROUPS):
                rows = pl.ds(j * KV_GROUPS + g, hp, stride=CMP_STRIDE * KV_GROUPS)
                xs_ref[g, p * hp:(p + 1) * hp, j * HEAD_DIM:(j + 1) * HEAD_DIM] = pg_ref[rows, :]
    _cmp_half_dots(xs_ref, pe_ref, w_ref, lo_ref, hi_ref)


def cmp_proj_pool(pool, page_ids, pe, w, pages_per_step=32):
    _, pr, _ = pool.shape
    hp = pr // (CMP_STRIDE * KV_GROUPS)
    kk = CMP_STRIDE * HEAD_DIM
    n = w.shape[2]
    npg = page_ids.shape[0]
    pps = pages_per_step
    assert npg % pps == 0 and hp == SUBLANES
    bm = hp * pps

    def page_spec(p):
        return pl.BlockSpec((None, pr, HEAD_DIM), lambda i, ids: (ids[i * pps + p], 0, 0))

    grid_spec = pltpu.PrefetchScalarGridSpec(
        num_scalar_prefetch=1,
        grid=(npg // pps,),
        in_specs=[page_spec(p) for p in range(pps)] + [
            pl.BlockSpec((2, kk), lambda i, ids: (0, 0)),
            pl.BlockSpec((2, kk, n), lambda i, ids: (0, 0, 0))],
        out_specs=[pl.BlockSpec((bm, KV_GROUPS * n), lambda i, ids: (i, 0))] * 2,
        scratch_shapes=[pltpu.VMEM((KV_GROUPS, bm, kk), F32)],
    )
    return pl.pallas_call(
        functools.partial(_cmp_proj_pool_body, pps=pps),
        out_shape=[_sds((npg * hp, KV_GROUPS * n), F32)] * 2,
        grid_spec=grid_spec,
        compiler_params=_cparams(("arbitrary",)),
        name="cmp_proj_pool",
    )(page_ids, *([pool] * pps), pe, w)


def _cmp_out_body(lo_ref, hi_ref, w2_ref, o_ref, ob_ref):
    s = _silu(lo_ref[...] + hi_ref[...])
    for g in range(KV_GROUPS):
        sl = slice(g * HEAD_DIM, (g + 1) * HEAD_DIM)
        y = jnp.dot(s[:, sl].astype(BF16), w2_ref[...], preferred_element_type=F32)
        o_ref[:, sl] = y
        ob_ref[:, sl] = y.astype(BF16)


def cmp_out(lo, hi_next, w2, bn=128):
    b, nc, n = lo.shape
    bn = min(bn, nc)
    spec = pl.BlockSpec((None, bn, n), lambda i, j: (i, j, 0))
    return pl.pallas_call(
        _cmp_out_body,
        out_shape=[_sds((b, nc, n), F32), _sds((b, nc, n), BF16)],
        grid=(b, nc // bn),
        in_specs=[spec, spec, pl.BlockSpec((HEAD_DIM, HEAD_DIM), lambda i, j: (0, 0))],
        out_specs=[spec, spec],
        compiler_params=_cparams(("parallel", "parallel")),
        name="cmp_out",
    )(lo, hi_next, w2)


def _cmp_weights(pe, w1):
    half = CMP_STRIDE * HEAD_DIM
    return pe.reshape(2, half), w1.reshape(2, half, -1).astype(BF16)


def _iota(shape, dim):
    return lax.broadcasted_iota(jnp.int32, shape, dim)


def _dot_nt(a, b):
    return lax.dot_general(a, b, (((1,), (1,)), ((), ())), preferred_element_type=F32)


def _split3_dot(p, c):
    p1 = p.astype(BF16)
    r1 = p - p1.astype(F32)
    p2 = r1.astype(BF16)
    p3 = (r1 - p2.astype(F32)).astype(BF16)
    return (jnp.dot(p1, c, preferred_element_type=F32) + jnp.dot(p2, c, preferred_element_type=F32)
            + jnp.dot(p3, c, preferred_element_type=F32))


def _cover(n_cmp, n_sel):
    n = _iota((n_cmp, n_sel), 0)
    j = _iota((n_cmp, n_sel), 1)
    lo = jnp.maximum(n * CMP_STRIDE, j * SEL_BLOCK)
    hi = jnp.minimum(n * CMP_STRIDE + CMP_BLOCK, (j + 1) * SEL_BLOCK)
    return (jnp.maximum(hi - lo, 0).astype(F32) * (1.0 / CMP_STRIDE)).astype(BF16)


def _head_slopes(g, hpg, shape, dim):
    h = g * hpg + _iota(shape, dim) + 1
    return jnp.exp2(h.astype(F32) * (-8.0 / (KV_GROUPS * hpg))) * LOG2E


def _flash_step(s3, v, m_ref, l_ref, acc_ref):
    hh, tq, tk = s3.shape
    m_prev = m_ref[...]
    m_new = jnp.maximum(m_prev, jnp.max(s3, axis=-1, keepdims=True))
    alpha = jnp.exp2(m_prev - m_new)
    p = jnp.exp2(s3 - m_new)
    l_ref[...] = alpha * l_ref[...] + jnp.sum(p, axis=-1, keepdims=True)
    pv = jnp.dot(p.reshape(hh * tq, tk).astype(BF16), v, preferred_element_type=F32)
    acc_ref[...] = alpha * acc_ref[...] + pv.reshape(hh, tq, HEAD_DIM)
    m_ref[...] = m_new


def _flash_init(m_ref, l_ref, acc_ref):
    m_ref[...] = jnp.full(m_ref.shape, NEG, F32)
    l_ref[...] = jnp.zeros(l_ref.shape, F32)
    acc_ref[...] = jnp.zeros(acc_ref.shape, F32)


def _select_rank(score, idx, n_candidates):
    rank = jnp.zeros(score.shape, jnp.int32)
    for i in range(n_candidates):
        si = score[:, i:i + 1]
        before = (si > score) | ((si == score) & (i < idx))
        rank = rank + before.astype(jnp.int32)
    return rank


def _cover_t(n_sel, n_cmp):
    j = _iota((n_sel, n_cmp), 0)
    n = _iota((n_sel, n_cmp), 1)
    lo = jnp.maximum(n * CMP_STRIDE, j * SEL_BLOCK)
    hi = jnp.minimum(n * CMP_STRIDE + CMP_BLOCK, (j + 1) * SEL_BLOCK)
    return (jnp.maximum(hi - lo, 0).astype(F32) * (1.0 / CMP_STRIDE)).astype(BF16)


def _nsa_prompt_body(slopes_ref, q_ref, gate_ref, kc_ref, vc_ref, sk_ref, sv_ref, wk_ref, wv_ref, o_ref,
                     vts_ref, vtw_ref, vct_ref, sd_ref, qt_ref, selt_ref, m_ref, l_ref, acc_ref,
                     ocmp_ref, oslc_ref, *, tq, tk, seq, n_cmp, hpg):
    g = pl.program_id(1)
    i = pl.program_id(2)
    t0 = i * tq
    n_sel = seq // SEL_BLOCK
    n_pick = min(N_SELECT, n_sel)
    nselp = selt_ref.shape[0]
    ncp = kc_ref.shape[0]
    bpt = tk // SEL_BLOCK

    def slope(h):
        return slopes_ref[g * hpg + h]

    def cols(h):
        return slice(h * tq, (h + 1) * tq)

    d0 = _iota((tk, tq), 1) - _iota((tk, tq), 0)

    @pl.when(i == 0)
    def _():
        vts_ref[...] = sv_ref[...].astype(F32).T.astype(BF16)
        vtw_ref[...] = wv_ref[...].astype(F32).T.astype(BF16)
        vct_ref[...] = vc_ref[...].astype(F32).T.astype(BF16)
        d0f = d0.astype(F32)
        for h in range(hpg):
            sd_ref[h] = slope(h) * d0f

    for h in range(hpg):
        qt_ref[:, cols(h)] = q_ref[:, h * HEAD_DIM:(h + 1) * HEAD_DIM].astype(F32).T.astype(BF16)

    t_row = t0 + _iota((1, tq), 1)

    n_col = _iota((ncp, 1), 0)
    dist_c = t_row - (n_col * CMP_STRIDE + (CMP_BLOCK - 1))
    valid_c = (dist_c >= 0) & (n_col < n_cmp)
    dist_cf = dist_c.astype(F32)
    valid_cf = valid_c.astype(F32)
    s_all = jnp.dot(kc_ref[...], qt_ref[...], preferred_element_type=F32)
    psum = jnp.zeros((ncp, tq), F32)
    ps = []
    for h in range(hpg):
        s = jnp.where(valid_c, s_all[:, cols(h)] - slope(h) * dist_cf, NEG)
        e = jnp.exp2(s - jnp.max(s, axis=0, keepdims=True))
        p = e * (valid_cf * (1.0 / jnp.sum(e, axis=0, keepdims=True)))
        psum = psum + p
        ps.append(p.astype(BF16))
    ocmp_ref[...] = jnp.dot(vct_ref[...], jnp.concatenate(ps, axis=1), preferred_element_type=F32)

    cover_t = _cover_t(nselp, ncp)
    p1 = psum.astype(BF16)
    r1 = psum - p1.astype(F32)
    p2 = r1.astype(BF16)
    p3 = (r1 - p2.astype(F32)).astype(BF16)
    score = (jnp.dot(cover_t, p1, preferred_element_type=F32)
             + jnp.dot(cover_t, p2, preferred_element_type=F32)
             + jnp.dot(cover_t, p3, preferred_element_type=F32))
    j_col = _iota((nselp, 1), 0)
    cur = t_row // SEL_BLOCK
    forced = (j_col < N_INIT_BLOCKS) | ((cur - j_col >= 0) & (cur - j_col < N_LOCAL_BLOCKS))
    score = jnp.where(forced, FORCE, score)
    score = jnp.where(j_col * SEL_BLOCK <= t_row, score, NEG)
    score = jnp.where(j_col < n_sel, score, -jnp.inf)
    rank = jnp.zeros(score.shape, jnp.int32)
    for c in range(n_sel):
        sc = score[c:c + 1, :]
        rank = rank + ((sc > score) | ((sc == score) & (c < j_col))).astype(jnp.int32)
    selt_ref[...] = ((rank < n_pick) & (j_col < n_sel)).astype(F32)

    def init():
        m_ref[...] = jnp.full(m_ref.shape, NEG, F32)
        l_ref[...] = jnp.zeros(l_ref.shape, F32)
        acc_ref[...] = jnp.zeros(acc_ref.shape, F32)

    def flash_tile(k_tile, vt_tile, ok, shift):
        s_all = jnp.dot(k_tile, qt_ref[...], preferred_element_type=F32)
        ps, alphas = [], []
        for h in range(hpg):
            u = slope(h) * shift
            s = jnp.where(ok, s_all[:, cols(h)] - sd_ref[h], NEG)
            m_prev = m_ref[:, cols(h)]
            m_new = jnp.maximum(m_prev, jnp.max(s, axis=0, keepdims=True) - u)
            alpha = jnp.exp2(m_prev - m_new)
            p = jnp.exp2(s - (m_new + u))
            l_ref[:, cols(h)] = alpha * l_ref[:, cols(h)] + jnp.sum(p, axis=0, keepdims=True)
            m_ref[:, cols(h)] = m_new
            ps.append(p.astype(BF16))
            alphas.append(alpha)
        pv = jnp.dot(vt_tile, jnp.concatenate(ps, axis=1), preferred_element_type=F32)
        acc_ref[...] = jnp.concatenate(alphas, axis=1) * acc_ref[...] + pv

    init()

    def slc_step(kt, carry):
        k0 = pl.multiple_of(kt * tk, tk)
        picked = jnp.concatenate(
            [jnp.broadcast_to(selt_ref[pl.ds(kt * bpt + b, 1), :], (SEL_BLOCK, tq)) for b in range(bpt)],
            axis=0)
        ok = (picked > 0.5) & (d0 >= k0 - t0)
        flash_tile(sk_ref[pl.ds(k0, tk), :], vts_ref[:, pl.ds(k0, tk)], ok, (t0 - k0).astype(F32))
        return carry

    lax.fori_loop(0, (t0 + tq + tk - 1) // tk, slc_step, 0)
    oslc_ref[...] = acc_ref[...] * (1.0 / l_ref[...])

    init()

    def win_step(kt, carry):
        k0 = pl.multiple_of(kt * tk, tk)
        ok = (d0 >= k0 - t0) & (d0 < WINDOW + k0 - t0)
        flash_tile(wk_ref[pl.ds(k0, tk), :], vtw_ref[:, pl.ds(k0, tk)], ok, (t0 - k0).astype(F32))
        return carry

    lax.fori_loop(jnp.maximum(t0 - WINDOW + 1, 0) // tk, (t0 + tq + tk - 1) // tk, win_step, 0)
    inv_l = 1.0 / l_ref[...]

    gate_t = gate_ref[...].T
    for h in range(hpg):
        def grow(br):
            c = br * hpg + h
            return gate_t[c:c + 1, :]
        o_t = (grow(0) * ocmp_ref[:, cols(h)] + grow(1) * oslc_ref[:, cols(h)]
               + (grow(2) * inv_l[:, cols(h)]) * acc_ref[:, cols(h)])
        o_ref[:, h * HEAD_DIM:(h + 1) * HEAD_DIM] = o_t.T.astype(o_ref.dtype)


def nsa_prompt(q, gate, kc, vc, kvb, slopes, batch, seq, n_cmp, tq=256, tk=256):
    m, hd = q.shape
    hpg = hd // HEAD_DIM // KV_GROUPS
    gw = hpg * HEAD_DIM
    ncp = kc.shape[1]
    nq = seq // tq
    n_sel = seq // SEL_BLOCK
    nselp = -(-n_sel // SUBLANES) * SUBLANES
    assert seq % tq == 0 and seq % tk == 0 and tk % SEL_BLOCK == 0

    def kv_spec(proj):
        return pl.BlockSpec((seq, HEAD_DIM), lambda b, g, i: (b, proj * KV_GROUPS + g))

    return pl.pallas_call(
        functools.partial(_nsa_prompt_body, tq=tq, tk=tk, seq=seq, n_cmp=n_cmp, hpg=hpg),
        out_shape=_sds((m, hd), BF16),
        grid=(batch, KV_GROUPS, nq),
        in_specs=[pl.BlockSpec(memory_space=pltpu.SMEM),
                  pl.BlockSpec((tq, gw), lambda b, g, i: (b * nq + i, g)),
                  pl.BlockSpec((tq, LANES), lambda b, g, i: (b * nq + i, g)),
                  pl.BlockSpec((None, ncp, HEAD_DIM), lambda b, g, i: (b, 0, g)),
                  pl.BlockSpec((None, ncp, HEAD_DIM), lambda b, g, i: (b, 0, g)),
                  kv_spec(2), kv_spec(3), kv_spec(4), kv_spec(5)],
        out_specs=pl.BlockSpec((tq, gw), lambda b, g, i: (b * nq + i, g)),
        scratch_shapes=[pltpu.VMEM((HEAD_DIM, seq), BF16),
                        pltpu.VMEM((HEAD_DIM, seq), BF16),
                        pltpu.VMEM((HEAD_DIM, ncp), BF16),
                        pltpu.VMEM((hpg, tk, tq), F32),
                        pltpu.VMEM((HEAD_DIM, hpg * tq), BF16),
                        pltpu.VMEM((nselp, tq), F32),
                        pltpu.VMEM((1, hpg * tq), F32),
                        pltpu.VMEM((1, hpg * tq), F32),
                        pltpu.VMEM((HEAD_DIM, hpg * tq), F32),
                        pltpu.VMEM((HEAD_DIM, hpg * tq), F32),
                        pltpu.VMEM((HEAD_DIM, hpg * tq), F32)],
        compiler_params=_cparams(("parallel", "parallel", "arbitrary")),
        name="nsa_prompt",
    )(slopes, q, gate, kc, vc, kvb, kvb, kvb, kvb)


def _stack_heads(q, g, hpg):
    return jnp.concatenate(
        [q[:, (g * hpg + h) * HEAD_DIM:(g * hpg + h + 1) * HEAD_DIM] for h in range(hpg)],
        axis=0).astype(BF16)


def _nsa_sample_cmp_win_body(q_ref, kc_ref, vc_ref, cwk_ref, cwv_ref, nwk_ref, nwv_ref,
                             ocmp_ref, owin_ref, sel_ref, *, s, past, n_cmp, n_sel, hpg):
    q = q_ref[...].astype(F32)
    ncp = kc_ref.shape[0]
    nbp = sel_ref.shape[-1]
    wbuf = cwk_ref.shape[0]
    n_pick = min(N_SELECT, n_sel)
    pos = past + _iota((s, 1), 0)
    cover = _cover(ncp, nbp)
    n_row = _iota((1, ncp), 1)
    dist_c = pos - (n_row * CMP_STRIDE + (CMP_BLOCK - 1))
    valid_c = (dist_c >= 0) & (n_row < n_cmp)
    j_row = _iota((1, nbp), 1)
    cur = pos // SEL_BLOCK
    forced = (j_row < N_INIT_BLOCKS) | ((cur - j_row >= 0) & (cur - j_row < N_LOCAL_BLOCKS))
    pad_rows = LANES - s
    d_w1 = pos - (past - wbuf + _iota((1, wbuf), 1))
    ok_w1 = (d_w1 >= 0) & (d_w1 < WINDOW) & (past - wbuf + _iota((1, wbuf), 1) >= 0)
    d_w2 = pos - (past + _iota((1, LANES), 1))
    ok_w2 = (d_w2 >= 0) & (d_w2 < WINDOW) & (_iota((1, LANES), 1) < s)

    for g in range(KV_GROUPS):
        gs = slice(g * HEAD_DIM, (g + 1) * HEAD_DIM)
        qs = _stack_heads(q, g, hpg)
        slope3 = _head_slopes(g, hpg, (hpg, 1, 1), 0)

        s3 = _dot_nt(qs, kc_ref[:, gs]).reshape(hpg, s, ncp)
        s3 = jnp.where(valid_c[None], s3 - slope3 * dist_c.astype(F32)[None], NEG)
        e = jnp.exp2(s3 - jnp.max(s3, axis=-1, keepdims=True))
        p3 = e / jnp.sum(e, axis=-1, keepdims=True) * valid_c.astype(F32)[None]
        ocmp_ref[g] = jnp.dot(p3.reshape(hpg * s, ncp).astype(BF16), vc_ref[:, gs],
                              preferred_element_type=F32)

        score = _split3_dot(jnp.sum(p3, axis=0), cover)
        score = jnp.where(forced, FORCE, score)
        score = jnp.where(j_row * SEL_BLOCK <= pos, score, NEG)
        score = jnp.where(j_row < n_sel, score, -jnp.inf)
        sel = (_select_rank(score, j_row, n_sel) < n_pick) & (j_row < n_sel)
        sel_ref[g] = sel.astype(F32)

        zpad = jnp.zeros((pad_rows, HEAD_DIM), F32)
        k2 = jnp.concatenate([nwk_ref[:, gs], zpad], axis=0).astype(BF16)
        v2 = jnp.concatenate([nwv_ref[:, gs], zpad], axis=0).astype(BF16)
        s1 = _dot_nt(qs, cwk_ref[:, gs].astype(BF16)).reshape(hpg, s, wbuf)
        s2 = _dot_nt(qs, k2).reshape(hpg, s, LANES)
        s1 = jnp.where(ok_w1[None], s1 - slope3 * d_w1.astype(F32)[None], NEG)
        s2 = jnp.where(ok_w2[None], s2 - slope3 * d_w2.astype(F32)[None], NEG)
        mx = jnp.maximum(jnp.max(s1, axis=-1, keepdims=True), jnp.max(s2, axis=-1, keepdims=True))
        p1 = jnp.exp2(s1 - mx)
        p2 = jnp.exp2(s2 - mx)
        den = jnp.sum(p1, axis=-1, keepdims=True) + jnp.sum(p2, axis=-1, keepdims=True)
        ow = (jnp.dot(p1.reshape(hpg * s, wbuf).astype(BF16), cwv_ref[:, gs].astype(BF16),
                      preferred_element_type=F32)
              + jnp.dot(p2.reshape(hpg * s, LANES).astype(BF16), v2, preferred_element_type=F32))
        owin_ref[g] = ow / den.reshape(hpg * s, 1)


def nsa_sample_cmp_win(q, kc, vc, cwk, cwv, kv_new, past, n_cmp, n_sel):
    b, s, hd = q.shape
    hpg = hd // HEAD_DIM // KV_GROUPS
    ncp = kc.shape[1]
    wbuf = cwk.shape[1]
    gd = KV_GROUPS * HEAD_DIM
    nbp = -(-n_sel // LANES) * LANES
    o_shape = _sds((b, KV_GROUPS, hpg * s, HEAD_DIM), F32)
    o_spec = pl.BlockSpec((None, KV_GROUPS, hpg * s, HEAD_DIM), lambda i: (i, 0, 0, 0))
    return pl.pallas_call(
        functools.partial(_nsa_sample_cmp_win_body, s=s, past=past, n_cmp=n_cmp, n_sel=n_sel, hpg=hpg),
        out_shape=[o_shape, o_shape, _sds((b, KV_GROUPS, s, nbp), F32)],
        grid=(b,),
        in_specs=[pl.BlockSpec((None, s, hd), lambda i: (i, 0, 0)),
                  pl.BlockSpec((None, ncp, gd), lambda i: (i, 0, 0)),
                  pl.BlockSpec((None, ncp, gd), lambda i: (i, 0, 0)),
                  pl.BlockSpec((None, wbuf, gd), lambda i: (i, 0, 0)),
                  pl.BlockSpec((None, wbuf, gd), lambda i: (i, 0, 0)),
                  pl.BlockSpec((None, s, gd), lambda i: (i, 0, 4)),
                  pl.BlockSpec((None, s, gd), lambda i: (i, 0, 5))],
        out_specs=[o_spec, o_spec,
                   pl.BlockSpec((None, KV_GROUPS, s, nbp), lambda i: (i, 0, 0, 0))],
        compiler_params=_cparams(("parallel",)),
        name="nsa_sample_cmp_win",
    )(q, kc, vc, cwk, cwv, kv_new, kv_new)


def _nsa_sample_slc_body(*refs, s, past, hpg, pps, n_chunks):
    ids_ref = refs[0]
    del ids_ref
    kp_refs = refs[1:1 + pps]
    vp_refs = refs[1 + pps:1 + 2 * pps]
    (q_ref, sel_ref, nk_ref, nv_ref, gate_ref, ocmp_ref, owin_ref, o_ref,
     qs_ref, m_ref, l_ref, acc_ref) = refs[1 + 2 * pps:]
    c = pl.program_id(1)
    nbp = sel_ref.shape[-1]
    tk = pps * PAGE_SIZE
    pos = past + _iota((s, 1), 0)

    @pl.when(c == 0)
    def _():
        q = q_ref[...].astype(F32)
        for g in range(KV_GROUPS):
            qs_ref[g] = _stack_heads(q, g, hpg)
            _flash_init(m_ref.at[g], l_ref.at[g], acc_ref.at[g])

    k0 = c * tk
    d = pos - (k0 + _iota((1, tk), 1))
    blk = (k0 + _iota((nbp, tk), 1)) // SEL_BLOCK
    expand = (_iota((nbp, tk), 0) == blk).astype(BF16)
    for g in range(KV_GROUPS):
        gs = slice(g * HEAD_DIM, (g + 1) * HEAD_DIM)
        slope3 = _head_slopes(g, hpg, (hpg, 1, 1), 0)
        grp = pl.ds(g, PAGE_SIZE, stride=KV_GROUPS)
        kk = jnp.concatenate([r[grp, :] for r in kp_refs], axis=0).astype(BF16)
        vv = jnp.concatenate([r[grp, :] for r in vp_refs], axis=0).astype(BF16)
        picked = jnp.dot(sel_ref[g].astype(BF16), expand, preferred_element_type=F32)
        ok = (picked > 0.5) & (d >= 0)
        sc = _dot_nt(qs_ref[g], kk).reshape(hpg, s, tk)
        sc = jnp.where(ok[None], sc - slope3 * d.astype(F32)[None], NEG)
        _flash_step(sc, vv, m_ref.at[g], l_ref.at[g], acc_ref.at[g])

    @pl.when(c == n_chunks - 1)
    def _():
        jn = past // SEL_BLOCK
        d2 = pos - (past + _iota((1, LANES), 1))
        in_new = (d2 >= 0) & (_iota((1, LANES), 1) < s)
        zpad = jnp.zeros((LANES - s, HEAD_DIM), F32)
        gate = gate_ref[...]
        for g in range(KV_GROUPS):
            gs = slice(g * HEAD_DIM, (g + 1) * HEAD_DIM)
            slope3 = _head_slopes(g, hpg, (hpg, 1, 1), 0)
            k2 = jnp.concatenate([nk_ref[:, gs], zpad], axis=0).astype(BF16)
            v2 = jnp.concatenate([nv_ref[:, gs], zpad], axis=0).astype(BF16)
            ok = in_new & (sel_ref[g][:, jn:jn + 1] > 0.5)
            sc = _dot_nt(qs_ref[g], k2).reshape(hpg, s, LANES)
            sc = jnp.where(ok[None], sc - slope3 * d2.astype(F32)[None], NEG)
            _flash_step(sc, v2, m_ref.at[g], l_ref.at[g], acc_ref.at[g])
            o_slc = acc_ref[g] / l_ref[g]
            o_cmp = ocmp_ref[g].reshape(hpg, s, HEAD_DIM)
            o_win = owin_ref[g].reshape(hpg, s, HEAD_DIM)
            for h in range(hpg):
                def gcol(br):
                    col = g * LANES + br * hpg + h
                    return gate[:, col:col + 1]
                o = gcol(0) * o_cmp[h] + gcol(1) * o_slc[h] + gcol(2) * o_win[h]
                hs = (g * hpg + h) * HEAD_DIM
                o_ref[:, hs:hs + HEAD_DIM] = o


def nsa_sample_slc(q, sel, kv_new, gate, o_cmp, o_win, pool_k, pool_v, page_table, past,
                   pages_per_step=4):
    b, s, hd = q.shape
    hpg = hd // HEAD_DIM // KV_GROUPS
    gd = KV_GROUPS * HEAD_DIM
    npages = page_table.shape[1]
    pps = pages_per_step
    assert npages % pps == 0 and s <= SEL_BLOCK and past % SEL_BLOCK == 0
    n_chunks = npages // pps
    nbp = sel.shape[-1]

    def page_spec(p):
        return pl.BlockSpec((None, PAGE_SIZE * KV_GROUPS, HEAD_DIM),
                            lambda i, c, ids: (ids[i * npages + c * pps + p], 0, 0))

    def per_b(shape):
        nd = len(shape)
        return pl.BlockSpec((None,) + shape, lambda i, c, ids: (i,) + (0,) * nd)

    o4 = (KV_GROUPS, hpg * s, HEAD_DIM)
    grid_spec = pltpu.PrefetchScalarGridSpec(
        num_scalar_prefetch=1,
        grid=(b, n_chunks),
        in_specs=[page_spec(p) for p in range(pps)] + [page_spec(p) for p in range(pps)] + [
            per_b((s, hd)), per_b((KV_GROUPS, s, nbp)),
            pl.BlockSpec((None, s, gd), lambda i, c, ids: (i, 0, 2)),
            pl.BlockSpec((None, s, gd), lambda i, c, ids: (i, 0, 3)),
            per_b((s, KV_GROUPS * LANES)), per_b(o4), per_b(o4)],
        out_specs=per_b((s, hd)),
        scratch_shapes=[pltpu.VMEM((KV_GROUPS, hpg * s, HEAD_DIM), BF16),
                        pltpu.VMEM((KV_GROUPS, hpg, s, 1), F32),
                        pltpu.VMEM((KV_GROUPS, hpg, s, 1), F32),
                        pltpu.VMEM((KV_GROUPS, hpg, s, HEAD_DIM), F32)],
    )
    return pl.pallas_call(
        functools.partial(_nsa_sample_slc_body, s=s, past=past, hpg=hpg, pps=pps, n_chunks=n_chunks),
        out_shape=_sds((b, s, hd), F32),
        grid_spec=grid_spec,
        compiler_params=_cparams(("parallel", "arbitrary")),
        name="nsa_sample_slc",
    )(page_table.reshape(-1), *([pool_k] * pps), *([pool_v] * pps),
      q, sel, kv_new, kv_new, gate, o_cmp, o_win)


def _gate_weight(w_qg, hd, hpg):
    wg = w_qg[:, hd:].reshape(-1, N_BRANCH, KV_GROUPS, hpg).transpose(0, 2, 1, 3)
    wg = wg.reshape(-1, KV_GROUPS, N_BRANCH * hpg)
    wg = jnp.pad(wg, ((0, 0), (0, 0), (0, LANES - N_BRANCH * hpg)))
    return wg.reshape(-1, KV_GROUPS * LANES).astype(BF16)


def kernel(x_prompt, x_sample, state_conv_a, state_ffn_conv, cache_cmp_k, cache_cmp_v, cache_slc_k,
           cache_slc_v, cache_win_k, cache_win_v, page_table, g_attn, g_ffn, g_final, w_pw1, w_dw_a,
           ln_a_g, ln_a_b, w_pw2, g_kv, w_kv, cmp_pe_k, cmp_w1_k, cmp_w2_k, cmp_pe_v, cmp_w1_v,
           cmp_w2_v, w_qg, w_o, w_up, w_dw_f, w_down):
    bp, tp, d = x_prompt.shape
    bs, ts, _ = x_sample.shape
    depth = g_attn.shape[0]
    n_a = w_pw1.shape[0]
    assert depth == n_a + 1, "one NSA layer after the convolution layers"
    f = w_down.shape[1]
    hd = w_o.shape[1]
    hpg = hd // HEAD_DIM // KV_GROUPS
    gd = KV_GROUPS * HEAD_DIM
    n_pages = page_table.shape[1]
    past = n_pages * PAGE_SIZE
    scale = HEAD_DIM ** -0.5
    mp, ms = bp * tp, bs * ts
    kconv = w_dw_a.shape[1]
    kffn = w_dw_f.shape[1]

    xp = x_prompt.reshape(mp, d)
    xs = x_sample.reshape(ms, d)
    conv_p, conv_s, ffn_p, ffn_s = [], [], [], []

    w_down_b = w_down.astype(BF16)

    def conv_ffn_both(xp, xs, layer):
        hp = rmsnorm(xp, g_ffn[layer], BF16)
        hs = rmsnorm(xs, g_ffn[layer], BF16)
        act, sta, stb, usa, usb = ffn_up(hp, hs, tp, w_up, layer, w_dw_f)
        ffn_p.append(jnp.concatenate([sta, stb], axis=-1)[:, STATE_ROWS - (kffn - 1):])
        xp = mm(act, w_down_b, layer=layer, bm=512, bn=512, resid=xp, vmem_mib=60, name="ffn_down")
        up_s = jnp.concatenate([usa, usb], axis=-1).reshape(bs, ts, 2 * f)
        xx = jnp.concatenate([state_ffn_conv[layer], up_s], axis=1)
        ffn_s.append(xx[:, -(kffn - 1):])
        act_s = ffn_gate_sample(xx, ts, w_dw_f[layer]).reshape(ms, f).astype(BF16)
        xs = mm(act_s, w_down_b, layer=layer, bm=ms, bn=512, resid=xs, vmem_mib=60, name="ffn_down_sample")
        return xp, xs

    for a in range(n_a):
        hp = rmsnorm(xp, g_attn[a], BF16)
        hs = rmsnorm(xs, g_attn[a], BF16)
        up, us = mm2(hp, hs, w_pw1, layer=a, bn=256, glu=True, name="pw1_glu")
        conv_p.append(up.reshape(bp, tp, d)[:, -(kconv - 1):])
        cp = conv_ln_prompt(up, tp, w_dw_a[a], ln_a_g[a], ln_a_b[a])
        xx = jnp.concatenate([state_conv_a[a], us.reshape(bs, ts, d)], axis=1)
        conv_s.append(xx[:, -(kconv - 1):])
        cs = conv_ln_sample(xx, ts, w_dw_a[a], ln_a_g[a], ln_a_b[a]).reshape(ms, d).astype(BF16)
        xp, xs = mm2(cp, cs, w_pw2, layer=a, resid=(xp, xs), name="pw2")

        xp, xs = conv_ffn_both(xp, xs, a)

    wkv = w_kv.astype(BF16)
    kv_rows_p, kvb_p = kv_proj(rmsnorm(xp, g_kv, BF16), wkv)
    kv_p4 = [r.reshape(bp, tp, KV_GROUPS, HEAD_DIM) for r in kv_rows_p]
    kv_s = mm(rmsnorm(xs, g_kv, BF16), wkv, bm=ms, bn=512, name="kv_sample")
    kv_s5 = kv_s.reshape(bs, ts, 2 * N_BRANCH, KV_GROUPS, HEAD_DIM)
    kv_s3 = kv_s.reshape(bs, ts, 2 * N_BRANCH * gd)

    half_w = CMP_STRIDE * gd
    nhp = tp // CMP_STRIDE
    n_cmp_p = nhp - 1
    assert tp % CMP_STRIDE == 0 and ts <= CMP_STRIDE and past % PAGE_SIZE == 0
    nhs = past // CMP_STRIDE
    n_cmp_s = nhs
    page_ids = page_table.reshape(-1)
    kcs = []
    for proj, (pe, w1c, w2c, pool) in enumerate([(cmp_pe_k, cmp_w1_k, cmp_w2_k, cache_cmp_k),
                                                 (cmp_pe_v, cmp_w1_v, cmp_w2_v, cache_cmp_v)]):
        pebig, wbig = _cmp_weights(pe, w1c)
        w2b = w2c.astype(BF16)
        rows_per_page = PAGE_SIZE * KV_GROUPS
        lo, hi = cmp_proj_pool(kv_rows_p[proj].reshape(-1, rows_per_page, HEAD_DIM),
                               jnp.arange(mp // PAGE_SIZE, dtype=jnp.int32), pebig, wbig,
                               pages_per_step=tp // PAGE_SIZE)
        lo = lo.reshape(bp, nhp, gd)
        hi = hi.reshape(bp, nhp, gd)
        hi_next = jnp.concatenate([hi[:, 1:], jnp.zeros((bp, 1, gd), F32)], axis=1)
        _, c_p = cmp_out(lo, hi_next, w2b)

        lo_h, hi_h = cmp_proj_pool(pool.reshape(pool.shape[0], PAGE_SIZE * KV_GROUPS, HEAD_DIM),
                                   page_ids, pebig, wbig, pages_per_step=min(32, bs * n_pages))
        new_rows = jnp.pad(kv_s3[:, :, proj * gd:(proj + 1) * gd], ((0, 0), (0, CMP_STRIDE - ts), (0, 0)))
        new_halves = new_rows.reshape(bs, CMP_STRIDE, KV_GROUPS, HEAD_DIM).transpose(2, 0, 1, 3)
        _, hi_n = cmp_proj_halves(new_halves.reshape(KV_GROUPS, bs, CMP_STRIDE * HEAD_DIM), pebig, wbig)
        hi_next_s = jnp.concatenate([hi_h.reshape(bs, nhs, gd)[:, 1:], hi_n[:, None]], axis=1)
        _, c_s = cmp_out(lo_h.reshape(bs, nhs, gd), hi_next_s, w2b)
        kcs.append((c_p, c_s))
    (kc_p, kc_s), (vc_p, vc_s) = kcs

    b_l = n_a
    wg = _gate_weight(w_qg[0], hd, hpg)
    hp = rmsnorm(xp, g_attn[b_l], BF16)
    hs = rmsnorm(xs, g_attn[b_l], BF16)
    q_p, q_s = mm2(hp, hs, w_qg, layer=0, n=hd, scale=scale * LOG2E, out_dtype=BF16, name="q_proj")
    gate_p, gate_s = mm2(hp, hs, wg, act="sigmoid", name="gate_proj")
    n_heads = hd // HEAD_DIM
    slopes = jnp.exp2(-8.0 * jnp.arange(1, n_heads + 1, dtype=F32) / n_heads) * LOG2E
    o_p = nsa_prompt(q_p, gate_p, kc_p, vc_p, kvb_p, slopes, bp, tp, n_cmp_p)

    q_s3 = q_s.reshape(bs, ts, hd)
    n_sel_s = past // SEL_BLOCK + -(-ts // SEL_BLOCK)
    wbuf = cache_win_k.shape[1]
    o_cmp_s, o_win_s, sel_s = nsa_sample_cmp_win(
        q_s3, kc_s, vc_s, cache_win_k.reshape(bs, wbuf, gd), cache_win_v.reshape(bs, wbuf, gd),
        kv_s3, past, n_cmp_s, n_sel_s)
    o_s = nsa_sample_slc(q_s3, sel_s, kv_s3, gate_s.reshape(bs, ts, KV_GROUPS * LANES), o_cmp_s, o_win_s,
                         cache_slc_k.reshape(-1, PAGE_SIZE * KV_GROUPS, HEAD_DIM),
                         cache_slc_v.reshape(-1, PAGE_SIZE * KV_GROUPS, HEAD_DIM),
                         page_table, past, pages_per_step=min(16, n_pages))
    xp, xs = mm2(o_p, o_s.reshape(ms, hd).astype(BF16), w_o, layer=0, resid=(xp, xs), name="o_proj")

    xp, xs = conv_ffn_both(xp, xs, b_l)

    y_prompt = rmsnorm(xp, g_final, F32).reshape(bp, tp, d)
    y_sample = rmsnorm(xs, g_final, F32).reshape(bs, ts, d)
    keep = min(WINDOW, tp)
    win_k_s = jnp.concatenate([cache_win_k, kv_s5[:, :, 4]], axis=1)[:, -wbuf:]
    win_v_s = jnp.concatenate([cache_win_v, kv_s5[:, :, 5]], axis=1)[:, -wbuf:]
    return (y_prompt, y_sample, jnp.stack(conv_p), jnp.stack(conv_s), jnp.stack(ffn_p), jnp.stack(ffn_s),
            kv_p4[0], kv_p4[1], kv_p4[2], kv_p4[3],
            kv_s5[:, :, 0], kv_s5[:, :, 1], kv_s5[:, :, 2], kv_s5[:, :, 3],
            kv_p4[4][:, -keep:], kv_p4[5][:, -keep:], win_k_s, win_v_s)
```

```python
import functools

import jax
import jax.numpy as jnp
from jax import lax
from jax.experimental import pallas as pl
from jax.experimental.pallas import tpu as pltpu

F32 = jnp.float32
BF16 = jnp.bfloat16

HEAD_DIM = 128
KV_GROUPS = 4
N_BRANCH = 3
CMP_BLOCK = 32
CMP_STRIDE = 16
SEL_BLOCK = 64
N_SELECT = 16
N_INIT_BLOCKS = 1
N_LOCAL_BLOCKS = 2
WINDOW = 512
PAGE_SIZE = 128
EPS = 1e-6
NEG = -1e30
FORCE = 1e9

LOG2E = 1.4426950408889634
LANES = 128
BF16_ROWS = 16
MIB = 1024 * 1024


def _cparams(sem, vmem_mib=48):
    return pltpu.CompilerParams(dimension_semantics=sem, vmem_limit_bytes=vmem_mib * MIB)


def _sds(shape, dtype):
    return jax.ShapeDtypeStruct(shape, dtype)


def _silu(x):
    return x * jax.nn.sigmoid(x)


def _rmsnorm_body(x_ref, g_ref, o_ref):
    x = x_ref[...]
    y = x * lax.rsqrt(jnp.mean(x * x, axis=-1, keepdims=True) + EPS)
    o_ref[...] = (y * g_ref[...]).astype(o_ref.dtype)


def rmsnorm(x, g, out_dtype, bm=512):
    m, d = x.shape
    bm = min(bm, m)
    return pl.pallas_call(
        _rmsnorm_body,
        out_shape=_sds((m, d), out_dtype),
        grid=(m // bm,),
        in_specs=[pl.BlockSpec((bm, d), lambda i: (i, 0)),
                  pl.BlockSpec((1, d), lambda i: (0, 0))],
        out_specs=pl.BlockSpec((bm, d), lambda i: (i, 0)),
        compiler_params=_cparams(("parallel",)),
        name="rmsnorm",
    )(x, g.reshape(1, d))


def _mm_body(*refs, glu, has_resid, scale, act, n_out):
    x_ref = refs[0]
    pos = 1
    x = x_ref[...]
    acc = jnp.dot(x, refs[pos][...], preferred_element_type=F32)
    pos += 1
    if glu:
        gate = jnp.dot(x, refs[pos][...], preferred_element_type=F32)
        pos += 1
        acc = acc * jax.nn.sigmoid(gate)
    if scale is not None:
        acc = acc * scale
    if act == "sigmoid":
        acc = jax.nn.sigmoid(acc)
    if has_resid:
        acc = acc + refs[pos][...]
        pos += 1
    for o_ref in refs[pos:pos + n_out]:
        o_ref[...] = acc.astype(o_ref.dtype)


def mm(x, w, *, bm, bn, layer=None, glu=False, resid=None, scale=None, act=None,
       out_dtypes=(F32,), vmem_mib=48, name="mm"):
    m, k = x.shape
    n = w.shape[-1] // 2 if glu else w.shape[-1]
    bm = min(bm, m)
    bn = min(bn, n)
    assert m % bm == 0 and n % bn == 0
    nb = n // bn
    if w.ndim == 3:
        def w_spec(off):
            return pl.BlockSpec((None, k, bn), lambda i, j: (layer, 0, j + off))
    else:
        def w_spec(off):
            return pl.BlockSpec((k, bn), lambda i, j: (0, j + off))
    in_specs = [pl.BlockSpec((bm, k), lambda i, j: (i, 0)), w_spec(0)]
    args = [x, w]
    if glu:
        in_specs.append(w_spec(nb))
        args.append(w)
    if resid is not None:
        in_specs.append(pl.BlockSpec((bm, bn), lambda i, j: (i, j)))
        args.append(resid)
    outs = pl.pallas_call(
        functools.partial(_mm_body, glu=glu, has_resid=resid is not None, scale=scale,
                          act=act, n_out=len(out_dtypes)),
        out_shape=[_sds((m, n), dt) for dt in out_dtypes],
        grid=(m // bm, n // bn),
        in_specs=in_specs,
        out_specs=[pl.BlockSpec((bm, bn), lambda i, j: (i, j)) for _ in out_dtypes],
        compiler_params=_cparams(("parallel", "arbitrary"), vmem_mib),
        name=name,
    )(*args)
    return outs[0] if len(out_dtypes) == 1 else outs


def _mm2_body(*refs, glu, has_resid, scale, act, ni):
    it = iter(refs)
    xp_ref, xs_ref, wa_ref = next(it), next(it), next(it)
    wb_ref = next(it) if glu else None
    rp_ref, rs_ref = (next(it), next(it)) if has_resid else (None, None)
    yp_ref, ys_ref, wsa_ref = next(it), next(it), next(it)
    wsb_ref = next(it) if glu else None
    i = pl.program_id(1)

    @pl.when(i == 0)
    def _():
        wsa_ref[...] = wa_ref[...].astype(BF16)
        if glu:
            wsb_ref[...] = wb_ref[...].astype(BF16)

    def compute(x_ref, r_ref, y_ref):
        x = x_ref[...]
        acc = jnp.dot(x, wsa_ref[...], preferred_element_type=F32)
        if glu:
            acc = acc * jax.nn.sigmoid(jnp.dot(x, wsb_ref[...], preferred_element_type=F32))
        if scale is not None:
            acc = acc * scale
        if act == "sigmoid":
            acc = jax.nn.sigmoid(acc)
        if r_ref is not None:
            acc = acc + r_ref[...]
        y_ref[...] = acc.astype(y_ref.dtype)

    @pl.when(i < ni)
    def _():
        compute(xp_ref, rp_ref, yp_ref)

    @pl.when(i == ni)
    def _():
        compute(xs_ref, rs_ref, ys_ref)


def mm2(xp, xs, w, *, n=None, layer=None, bm=1024, bn=512, glu=False, resid=None, scale=None, act=None,
        out_dtype=F32, vmem_mib=56, name="mm2"):
    mp, k = xp.shape
    ms = xs.shape[0]
    nw = w.shape[-1]
    if n is None:
        n = nw // 2 if glu else nw
    bm = min(bm, mp)
    bn = min(bn, n)
    assert mp % bm == 0 and n % bn == 0
    ni, nb = mp // bm, n // bn
    if w.ndim == 3:
        def w_spec(off):
            return pl.BlockSpec((None, k, bn), lambda j, i: (layer, 0, j + off))
    else:
        def w_spec(off):
            return pl.BlockSpec((k, bn), lambda j, i: (0, j + off))
    prow = lambda j, i: (jnp.minimum(i, ni - 1), 0)
    pblk = lambda j, i: (jnp.minimum(i, ni - 1), j)
    in_specs = [pl.BlockSpec((bm, k), prow), pl.BlockSpec((ms, k), lambda j, i: (0, 0)), w_spec(0)]
    args = [xp, xs, w]
    if glu:
        in_specs.append(w_spec(nb))
        args.append(w)
    if resid is not None:
        in_specs += [pl.BlockSpec((bm, bn), pblk), pl.BlockSpec((ms, bn), lambda j, i: (0, j))]
        args += list(resid)
    scratch = [pltpu.VMEM((k, bn), BF16)] * (2 if glu else 1)
    return pl.pallas_call(
        functools.partial(_mm2_body, glu=glu, has_resid=resid is not None, scale=scale, act=act, ni=ni),
        out_shape=[_sds((mp, n), out_dtype), _sds((ms, n), out_dtype)],
        grid=(nb, ni + 1),
        in_specs=in_specs,
        out_specs=[pl.BlockSpec((bm, bn), pblk), pl.BlockSpec((ms, bn), lambda j, i: (0, j))],
        scratch_shapes=scratch,
        compiler_params=_cparams(("parallel", "arbitrary"), vmem_mib),
        name=name,
    )(*args)


def _kv_proj_body(x_ref, w_ref, *o_refs, n_proj):
    j = pl.program_id(1)
    bm = x_ref.shape[0]
    acc = jnp.dot(x_ref[...], w_ref[...], preferred_element_type=F32)
    o_refs[n_proj][...] = acc.astype(BF16)
    for p in range(n_proj):
        @pl.when(j == p)
        def _(p=p):
            for g in range(KV_GROUPS):
                o_refs[p][pl.ds(g, bm, stride=KV_GROUPS), :] = acc[:, g * HEAD_DIM:(g + 1) * HEAD_DIM]


def kv_proj(x, w, bm=512):
    m, k = x.shape
    gd = KV_GROUPS * HEAD_DIM
    n_proj = w.shape[1] // gd
    assert m % bm == 0
    row_spec = pl.BlockSpec((bm * KV_GROUPS, HEAD_DIM), lambda i, j: (i, 0))
    outs = pl.pallas_call(
        functools.partial(_kv_proj_body, n_proj=n_proj),
        out_shape=[_sds((m * KV_GROUPS, HEAD_DIM), F32)] * n_proj + [_sds((m, n_proj * gd), BF16)],
        grid=(m // bm, n_proj),
        in_specs=[pl.BlockSpec((bm, k), lambda i, j: (i, 0)),
                  pl.BlockSpec((k, gd), lambda i, j: (0, j))],
        out_specs=[row_spec] * n_proj + [pl.BlockSpec((bm, gd), lambda i, j: (i, j))],
        compiler_params=_cparams(("parallel", "arbitrary")),
        name="kv_proj",
    )(x, w)
    return outs[:n_proj], outs[n_proj]


def _ln_silu(c, g, b):
    mu = jnp.mean(c, axis=-1, keepdims=True)
    xc = c - mu
    y = xc * lax.rsqrt(jnp.mean(xc * xc, axis=-1, keepdims=True) + EPS)
    return _silu(y * g + b)


CONV_HALO = 32


SUBLANES = 8


def _conv_ln_body(u_ref, halo_ref, w_ref, g_ref, b_ref, o_ref, xs_ref, sh_ref, c_ref,
                  *, bt, blocks_per_seq, kw, cw, rc):
    i = pl.program_id(0)
    d = u_ref.shape[1]
    keep = jnp.where(i % blocks_per_seq == 0, 0.0, 1.0)
    xs_ref[0:CONV_HALO, :] = halo_ref[...] * keep
    xs_ref[CONV_HALO:, :] = u_ref[...]
    base = CONV_HALO - (kw - 1)
    span = sh_ref.shape[1]

    def chunk(ci, carry):
        lanes = pl.ds(pl.multiple_of(ci * cw, cw), cw)
        for r in range(1, SUBLANES):
            sh_ref[r - 1] = xs_ref[r:r + span, lanes]
        for rb in range(bt // rc):
            acc = jnp.zeros((rc, cw), F32)
            for k in range(kw):
                a, r = divmod(base + k, SUBLANES)
                row0 = rb * rc + SUBLANES * a
                if r == 0:
                    x = xs_ref[row0:row0 + rc, lanes]
                else:
                    x = sh_ref[r - 1, row0:row0 + rc, :]
                acc = acc + w_ref[k:k + 1, lanes] * x
            c_ref[rb * rc:(rb + 1) * rc, lanes] = acc
        return carry

    lax.fori_loop(0, d // cw, chunk, 0)

    def ln(rb, carry):
        rows = pl.ds(pl.multiple_of(rb * rc, rc), rc)
        o_ref[rows, :] = _ln_silu(c_ref[rows, :], g_ref[...], b_ref[...]).astype(o_ref.dtype)
        return carry

    lax.fori_loop(0, bt // rc, ln, 0)


def conv_ln_prompt(u, seq, w_dw, ln_g, ln_b, bt=256, cw=512, rc=64):
    m, d = u.shape
    kw = w_dw.shape[0]
    assert kw - 1 <= CONV_HALO and seq % bt == 0 and bt % CONV_HALO == 0
    assert d % cw == 0 and bt % rc == 0
    r = bt // CONV_HALO
    span = bt + CONV_HALO - SUBLANES
    return pl.pallas_call(
        functools.partial(_conv_ln_body, bt=bt, blocks_per_seq=seq // bt, kw=kw, cw=cw, rc=rc),
        out_shape=_sds((m, d), BF16),
        grid=(m // bt,),
        in_specs=[pl.BlockSpec((bt, d), lambda i: (i, 0)),
                  pl.BlockSpec((CONV_HALO, d), lambda i: (jnp.maximum(i * r - 1, 0), 0)),
                  pl.BlockSpec((kw, d), lambda i: (0, 0)),
                  pl.BlockSpec((1, d), lambda i: (0, 0)),
                  pl.BlockSpec((1, d), lambda i: (0, 0))],
        out_specs=pl.BlockSpec((bt, d), lambda i: (i, 0)),
        scratch_shapes=[pltpu.VMEM((bt + CONV_HALO, d), F32),
                        pltpu.VMEM((SUBLANES - 1, span, cw), F32),
                        pltpu.VMEM((bt, d), F32)],
        compiler_params=_cparams(("parallel",)),
        name="conv_ln_prompt",
    )(u, u, w_dw, ln_g.reshape(1, d), ln_b.reshape(1, d))


def _conv_ln_sample_body(xx_ref, w_ref, g_ref, b_ref, o_ref, *, s, kw):
    acc = jnp.zeros((s, xx_ref.shape[-1]), F32)
    for k in range(kw):
        acc = acc + w_ref[k:k + 1, :] * xx_ref[k:k + s, :]
    o_ref[...] = _ln_silu(acc, g_ref[...], b_ref[...])


def conv_ln_sample(xx, s, w_dw, ln_g, ln_b):
    b, l, d = xx.shape
    kw = w_dw.shape[0]
    return pl.pallas_call(
        functools.partial(_conv_ln_sample_body, s=s, kw=kw),
        out_shape=_sds((b, s, d), F32),
        grid=(b,),
        in_specs=[pl.BlockSpec((None, l, d), lambda i: (i, 0, 0)),
                  pl.BlockSpec((kw, d), lambda i: (0, 0)),
                  pl.BlockSpec((1, d), lambda i: (0, 0)),
                  pl.BlockSpec((1, d), lambda i: (0, 0))],
        out_specs=pl.BlockSpec((None, s, d), lambda i: (i, 0, 0)),
        compiler_params=_cparams(("parallel",)),
        name="conv_ln_sample",
    )(xx, w_dw, ln_g.reshape(1, d), ln_b.reshape(1, d))


FFN_HALO = BF16_ROWS
STATE_ROWS = 8


def _ffn_up_rows_body(h_ref, halo_ref, wa_ref, wb_ref, dwa_ref, dwb_ref, act_ref, sta_ref, stb_ref,
                      hx_ref, *, bm, blocks_per_seq, kw):
    i = pl.program_id(0)

    @pl.when(pl.program_id(1) == 0)
    def _():
        keep = jnp.where(i % blocks_per_seq == 0, 0.0, 1.0)
        hx_ref[0:FFN_HALO, :] = (halo_ref[...].astype(F32) * keep).astype(BF16)
        hx_ref[FFN_HALO:, :] = h_ref[...]

    hx = hx_ref[...]
    ua = jnp.dot(hx, wa_ref[...], preferred_element_type=F32)
    ub = jnp.dot(hx, wb_ref[...], preferred_element_type=F32)

    def conv(u, dw_ref):
        c = jnp.zeros((bm, u.shape[1]), F32)
        for k in range(kw):
            off = FFN_HALO - (kw - 1) + k
            c = c + dw_ref[k:k + 1, :] * u[off:off + bm, :]
        return c

    act_ref[...] = (_silu(conv(ua, dwa_ref)) * conv(ub, dwb_ref)).astype(act_ref.dtype)
    sta_ref[...] = ua[FFN_HALO + bm - STATE_ROWS:, :]
    stb_ref[...] = ub[FFN_HALO + bm - STATE_ROWS:, :]


def ffn_up_rows(h, seq, w_up, layer, w_dw, bm=1024, bn=256, vmem_mib=48):
    m, d = h.shape
    f = w_up.shape[-1] // 2
    kw = w_dw.shape[1]
    assert seq % bm == 0 and f % bn == 0 and kw - 1 <= FFN_HALO
    bps = seq // bm
    r = bm // FFN_HALO
    nb = f // bn
    act, sta, stb = pl.pallas_call(
        functools.partial(_ffn_up_rows_body, bm=bm, blocks_per_seq=bps, kw=kw),
        out_shape=[_sds((m, f), BF16), _sds((m // bm, STATE_ROWS, f), F32),
                   _sds((m // bm, STATE_ROWS, f), F32)],
        grid=(m // bm, nb),
        in_specs=[pl.BlockSpec((bm, d), lambda i, j: (i, 0)),
                  pl.BlockSpec((FFN_HALO, d), lambda i, j: (jnp.maximum(i * r - 1, 0), 0)),
                  pl.BlockSpec((None, d, bn), lambda i, j: (layer, 0, j)),
                  pl.BlockSpec((None, d, bn), lambda i, j: (layer, 0, j + nb)),
                  pl.BlockSpec((None, kw, bn), lambda i, j: (layer, 0, j)),
                  pl.BlockSpec((None, kw, bn), lambda i, j: (layer, 0, j + nb))],
        out_specs=[pl.BlockSpec((bm, bn), lambda i, j: (i, j)),
                   pl.BlockSpec((None, STATE_ROWS, bn), lambda i, j: (i, 0, j)),
                   pl.BlockSpec((None, STATE_ROWS, bn), lambda i, j: (i, 0, j))],
        scratch_shapes=[pltpu.VMEM((bm + FFN_HALO, d), BF16)],
        compiler_params=_cparams(("parallel", "arbitrary"), vmem_mib),
        name="ffn_up_rows",
    )(h, h, w_up, w_up, w_dw, w_dw)
    return act, sta[bps - 1::bps], stb[bps - 1::bps]


def _ffn_gate_sample_body(xa_ref, xb_ref, dwa_ref, dwb_ref, o_ref, *, s, kw):
    def conv(x_ref, dw_ref):
        c = jnp.zeros((x_ref.shape[0], s, x_ref.shape[2]), F32)
        for k in range(kw):
            c = c + dw_ref[k:k + 1, :] * x_ref[:, k:k + s, :]
        return c

    o_ref[...] = _silu(conv(xa_ref, dwa_ref)) * conv(xb_ref, dwb_ref)


def ffn_gate_sample(xx, s, w_dw):
    b, l, f2 = xx.shape
    f = f2 // 2
    kw = w_dw.shape[0]
    return pl.pallas_call(
        functools.partial(_ffn_gate_sample_body, s=s, kw=kw),
        out_shape=_sds((b, s, f), F32),
        grid=(1,),
        in_specs=[pl.BlockSpec((b, l, f), lambda i: (0, 0, 0)),
                  pl.BlockSpec((b, l, f), lambda i: (0, 0, 1)),
                  pl.BlockSpec((kw, f), lambda i: (0, 0)),
                  pl.BlockSpec((kw, f), lambda i: (0, 1))],
        out_specs=pl.BlockSpec((b, s, f), lambda i: (0, 0, 0)),
        compiler_params=_cparams(("arbitrary",)),
        name="ffn_gate_sample",
    )(xx, xx, w_dw, w_dw)


def _cmp_half_dots(xs_ref, pe_ref, w_ref, lo_ref, hi_ref):
    for g in range(KV_GROUPS):
        x = xs_ref[g]
        sl = slice(g * HEAD_DIM, (g + 1) * HEAD_DIM)
        lo_ref[:, sl] = jnp.dot((x + pe_ref[0:1, :]).astype(BF16), w_ref[0], preferred_element_type=F32)
        hi_ref[:, sl] = jnp.dot((x + pe_ref[1:2, :]).astype(BF16), w_ref[1], preferred_element_type=F32)


def cmp_proj_halves(xs, pe, w):
    _, r, kk = xs.shape
    n = w.shape[2]
    return pl.pallas_call(
        _cmp_half_dots,
        out_shape=[_sds((r, KV_GROUPS * n), F32)] * 2,
        grid=(1,),
        in_specs=[pl.BlockSpec((KV_GROUPS, r, kk), lambda i: (0, 0, 0)),
                  pl.BlockSpec((2, kk), lambda i: (0, 0)),
                  pl.BlockSpec((2, kk, n), lambda i: (0, 0, 0))],
        out_specs=[pl.BlockSpec((r, KV_GROUPS * n), lambda i: (0, 0))] * 2,
        compiler_params=_cparams(("arbitrary",)),
        name="cmp_proj_halves",
    )(xs, pe, w)


def _cmp_proj_pool_body(*refs, pps):
    pg_refs = refs[1:1 + pps]
    pe_ref, w_ref, lo_ref, hi_ref, xs_ref = refs[1 + pps:]
    hp = pg_refs[0].shape[0] // (CMP_STRIDE * KV_GROUPS)
    for p, pg_ref in enumerate(pg_refs):
        for j in range(CMP_STRIDE):
            for g in range(KV_GROUPS):
                rows = pl.ds(j * KV_GROUPS + g, hp, stride=CMP_STRIDE * KV_GROUPS)
                xs_ref[g, p * hp:(p + 1) * hp, j * HEAD_DIM:(j + 1) * HEAD_DIM] = pg_ref[rows, :]
    _cmp_half_dots(xs_ref, pe_ref, w_ref, lo_ref, hi_ref)


def cmp_proj_pool(pool, page_ids, pe, w, pages_per_step=32):
    _, pr, _ = pool.shape
    hp = pr // (CMP_STRIDE * KV_GROUPS)
    kk = CMP_STRIDE * HEAD_DIM
    n = w.shape[2]
    npg = page_ids.shape[0]
    pps = pages_per_step
    assert npg % pps == 0 and hp == SUBLANES
    bm = hp * pps

    def page_spec(p):
        return pl.BlockSpec((None, pr, HEAD_DIM), lambda i, ids: (ids[i * pps + p], 0, 0))

    grid_spec = pltpu.PrefetchScalarGridSpec(
        num_scalar_prefetch=1,
        grid=(npg // pps,),
        in_specs=[page_spec(p) for p in range(pps)] + [
            pl.BlockSpec((2, kk), lambda i, ids: (0, 0)),
            pl.BlockSpec((2, kk, n), lambda i, ids: (0, 0, 0))],
        out_specs=[pl.BlockSpec((bm, KV_GROUPS * n), lambda i, ids: (i, 0))] * 2,
        scratch_shapes=[pltpu.VMEM((KV_GROUPS, bm, kk), F32)],
    )
    return pl.pallas_call(
        functools.partial(_cmp_proj_pool_body, pps=pps),
        out_shape=[_sds((npg * hp, KV_GROUPS * n), F32)] * 2,
        grid_spec=grid_spec,
        compiler_params=_cparams(("arbitrary",)),
        name="cmp_proj_pool",
    )(page_ids, *([pool] * pps), pe, w)


def _cmp_out_body(lo_ref, hi_ref, w2_ref, o_ref, ob_ref):
    s = _silu(lo_ref[...] + hi_ref[...])
    for g in range(KV_GROUPS):
        sl = slice(g * HEAD_DIM, (g + 1) * HEAD_DIM)
        y = jnp.dot(s[:, sl].astype(BF16), w2_ref[...], preferred_element_type=F32)
        o_ref[:, sl] = y
        ob_ref[:, sl] = y.astype(BF16)


def cmp_out(lo, hi_next, w2, bn=128):
    b, nc, n = lo.shape
    bn = min(bn, nc)
    spec = pl.BlockSpec((None, bn, n), lambda i, j: (i, j, 0))
    return pl.pallas_call(
        _cmp_out_body,
        out_shape=[_sds((b, nc, n), F32), _sds((b, nc, n), BF16)],
        grid=(b, nc // bn),
        in_specs=[spec, spec, pl.BlockSpec((HEAD_DIM, HEAD_DIM), lambda i, j: (0, 0))],
        out_specs=[spec, spec],
        compiler_params=_cparams(("parallel", "parallel")),
        name="cmp_out",
    )(lo, hi_next, w2)


def _cmp_weights(pe, w1):
    half = CMP_STRIDE * HEAD_DIM
    return pe.reshape(2, half), w1.reshape(2, half, -1).astype(BF16)


def _iota(shape, dim):
    return lax.broadcasted_iota(jnp.int32, shape, dim)


def _dot_nt(a, b):
    return lax.dot_general(a, b, (((1,), (1,)), ((), ())), preferred_element_type=F32)


def _split3_dot(p, c):
    p1 = p.astype(BF16)
    r1 = p - p1.astype(F32)
    p2 = r1.astype(BF16)
    p3 = (r1 - p2.astype(F32)).astype(BF16)
    return (jnp.dot(p1, c, preferred_element_type=F32) + jnp.dot(p2, c, preferred_element_type=F32)
            + jnp.dot(p3, c, preferred_element_type=F32))


def _cover(n_cmp, n_sel):
    n = _iota((n_cmp, n_sel), 0)
    j = _iota((n_cmp, n_sel), 1)
    lo = jnp.maximum(n * CMP_STRIDE, j * SEL_BLOCK)
    hi = jnp.minimum(n * CMP_STRIDE + CMP_BLOCK, (j + 1) * SEL_BLOCK)
    return (jnp.maximum(hi - lo, 0).astype(F32) * (1.0 / CMP_STRIDE)).astype(BF16)


def _head_slopes(g, hpg, shape, dim):
    h = g * hpg + _iota(shape, dim) + 1
    return jnp.exp2(h.astype(F32) * (-8.0 / (KV_GROUPS * hpg))) * LOG2E


def _flash_step(s3, v, m_ref, l_ref, acc_ref):
    hh, tq, tk = s3.shape
    m_prev = m_ref[...]
    m_new = jnp.maximum(m_prev, jnp.max(s3, axis=-1, keepdims=True))
    alpha = jnp.exp2(m_prev - m_new)
    p = jnp.exp2(s3 - m_new)
    l_ref[...] = alpha * l_ref[...] + jnp.sum(p, axis=-1, keepdims=True)
    pv = jnp.dot(p.reshape(hh * tq, tk).astype(BF16), v, preferred_element_type=F32)
    acc_ref[...] = alpha * acc_ref[...] + pv.reshape(hh, tq, HEAD_DIM)
    m_ref[...] = m_new


def _flash_init(m_ref, l_ref, acc_ref):
    m_ref[...] = jnp.full(m_ref.shape, NEG, F32)
    l_ref[...] = jnp.zeros(l_ref.shape, F32)
    acc_ref[...] = jnp.zeros(acc_ref.shape, F32)


def _select_rank(score, idx, n_candidates):
    rank = jnp.zeros(score.shape, jnp.int32)
    for i in range(n_candidates):
        si = score[:, i:i + 1]
        before = (si > score) | ((si == score) & (i < idx))
        rank = rank + before.astype(jnp.int32)
    return rank


def _cover_t(n_sel, n_cmp):
    j = _iota((n_sel, n_cmp), 0)
    n = _iota((n_sel, n_cmp), 1)
    lo = jnp.maximum(n * CMP_STRIDE, j * SEL_BLOCK)
    hi = jnp.minimum(n * CMP_STRIDE + CMP_BLOCK, (j + 1) * SEL_BLOCK)
    return (jnp.maximum(hi - lo, 0).astype(F32) * (1.0 / CMP_STRIDE)).astype(BF16)


def _nsa_prompt_body(slopes_ref, q_ref, gate_ref, kc_ref, vc_ref, sk_ref, sv_ref, wk_ref, wv_ref, o_ref,
                     vts_ref, vtw_ref, vct_ref, sd_ref, qt_ref, selt_ref, m_ref, l_ref, acc_ref,
                     ocmp_ref, oslc_ref, *, tq, tk, seq, n_cmp, hpg):
    g = pl.program_id(1)
    i = pl.program_id(2)
    t0 = i * tq
    n_sel = seq // SEL_BLOCK
    n_pick = min(N_SELECT, n_sel)
    nselp = selt_ref.shape[0]
    ncp = kc_ref.shape[0]
    bpt = tk // SEL_BLOCK

    def slope(h):
        return slopes_ref[g * hpg + h]

    def cols(h):
        return slice(h * tq, (h + 1) * tq)

    d0 = _iota((tk, tq), 1) - _iota((tk, tq), 0)

    @pl.when(i == 0)
    def _():
        vts_ref[...] = sv_ref[...].astype(F32).T.astype(BF16)
        vtw_ref[...] = wv_ref[...].astype(F32).T.astype(BF16)
        vct_ref[...] = vc_ref[...].astype(F32).T.astype(BF16)
        d0f = d0.astype(F32)
        for h in range(hpg):
            sd_ref[h] = slope(h) * d0f

    for h in range(hpg):
        qt_ref[:, cols(h)] = q_ref[:, h * HEAD_DIM:(h + 1) * HEAD_DIM].astype(F32).T.astype(BF16)

    t_row = t0 + _iota((1, tq), 1)

    n_col = _iota((ncp, 1), 0)
    dist_c = t_row - (n_col * CMP_STRIDE + (CMP_BLOCK - 1))
    valid_c = (dist_c >= 0) & (n_col < n_cmp)
    dist_cf = dist_c.astype(F32)
    valid_cf = valid_c.astype(F32)
    s_all = jnp.dot(kc_ref[...], qt_ref[...], preferred_element_type=F32)
    psum = jnp.zeros((ncp, tq), F32)
    ps = []
    for h in range(hpg):
        s = jnp.where(valid_c, s_all[:, cols(h)] - slope(h) * dist_cf, NEG)
        e = jnp.exp2(s - jnp.max(s, axis=0, keepdims=True))
        p = e * (valid_cf * (1.0 / jnp.sum(e, axis=0, keepdims=True)))
        psum = psum + p
        ps.append(p.astype(BF16))
    ocmp_ref[...] = jnp.dot(vct_ref[...], jnp.concatenate(ps, axis=1), preferred_element_type=F32)

    cover_t = _cover_t(nselp, ncp)
    p1 = psum.astype(BF16)
    r1 = psum - p1.astype(F32)
    p2 = r1.astype(BF16)
    p3 = (r1 - p2.astype(F32)).astype(BF16)
    score = (jnp.dot(cover_t, p1, preferred_element_type=F32)
             + jnp.dot(cover_t, p2, preferred_element_type=F32)
             + jnp.dot(cover_t, p3, preferred_element_type=F32))
    j_col = _iota((nselp, 1), 0)
    cur = t_row // SEL_BLOCK
    forced = (j_col < N_INIT_BLOCKS) | ((cur - j_col >= 0) & (cur - j_col < N_LOCAL_BLOCKS))
    score = jnp.where(forced, FORCE, score)
    score = jnp.where(j_col * SEL_BLOCK <= t_row, score, NEG)
    score = jnp.where(j_col < n_sel, score, -jnp.inf)
    rank = jnp.zeros(score.shape, jnp.int32)
    for c in range(n_sel):
        sc = score[c:c + 1, :]
        rank = rank + ((sc > score) | ((sc == score) & (c < j_col))).astype(jnp.int32)
    selt_ref[...] = ((rank < n_pick) & (j_col < n_sel)).astype(F32)

    def init():
        m_ref[...] = jnp.full(m_ref.shape, NEG, F32)
        l_ref[...] = jnp.zeros(l_ref.shape, F32)
        acc_ref[...] = jnp.zeros(acc_ref.shape, F32)

    def flash_tile(k_tile, vt_tile, ok, shift):
        s_all = jnp.dot(k_tile, qt_ref[...], preferred_element_type=F32)
        ps, alphas = [], []
        for h in range(hpg):
            u = slope(h) * shift
            s = jnp.where(ok, s_all[:, cols(h)] - sd_ref[h], NEG)
            m_prev = m_ref[:, cols(h)]
            m_new = jnp.maximum(m_prev, jnp.max(s, axis=0, keepdims=True) - u)
            alpha = jnp.exp2(m_prev - m_new)
            p = jnp.exp2(s - (m_new + u))
            l_ref[:, cols(h)] = alpha * l_ref[:, cols(h)] + jnp.sum(p, axis=0, keepdims=True)
            m_ref[:, cols(h)] = m_new
            ps.append(p.astype(BF16))
            alphas.append(alpha)
        pv = jnp.dot(vt_tile, jnp.concatenate(ps, axis=1), preferred_element_type=F32)
        acc_ref[...] = jnp.concatenate(alphas, axis=1) * acc_ref[...] + pv

    init()

    def slc_step(kt, carry):
        k0 = pl.multiple_of(kt * tk, tk)
        picked = jnp.concatenate(
            [jnp.broadcast_to(selt_ref[pl.ds(kt * bpt + b, 1), :], (SEL_BLOCK, tq)) for b in range(bpt)],
            axis=0)
        ok = (picked > 0.5) & (d0 >= k0 - t0)
        flash_tile(sk_ref[pl.ds(k0, tk), :], vts_ref[:, pl.ds(k0, tk)], ok, (t0 - k0).astype(F32))
        return carry

    lax.fori_loop(0, (t0 + tq + tk - 1) // tk, slc_step, 0)
    oslc_ref[...] = acc_ref[...] * (1.0 / l_ref[...])

    init()

    def win_step(kt, carry):
        k0 = pl.multiple_of(kt * tk, tk)
        ok = (d0 >= k0 - t0) & (d0 < WINDOW + k0 - t0)
        flash_tile(wk_ref[pl.ds(k0, tk), :], vtw_ref[:, pl.ds(k0, tk)], ok, (t0 - k0).astype(F32))
        return carry

    lax.fori_loop(jnp.maximum(t0 - WINDOW + 1, 0) // tk, (t0 + tq + tk - 1) // tk, win_step, 0)
    inv_l = 1.0 / l_ref[...]

    gate_t = gate_ref[...].T
    for h in range(hpg):
        def grow(br):
            c = br * hpg + h
            return gate_t[c:c + 1, :]
        o_t = (grow(0) * ocmp_ref[:, cols(h)] + grow(1) * oslc_ref[:, cols(h)]
               + (grow(2) * inv_l[:, cols(h)]) * acc_ref[:, cols(h)])
        o_ref[:, h * HEAD_DIM:(h + 1) * HEAD_DIM] = o_t.T.astype(o_ref.dtype)


def nsa_prompt(q, gate, kc, vc, kvb, slopes, batch, seq, n_cmp, tq=256, tk=256):
    m, hd = q.shape
    hpg = hd // HEAD_DIM // KV_GROUPS
    gw = hpg * HEAD_DIM
    ncp = kc.shape[1]
    nq = seq // tq
    n_sel = seq // SEL_BLOCK
    nselp = -(-n_sel // SUBLANES) * SUBLANES
    assert seq % tq == 0 and seq % tk == 0 and tk % SEL_BLOCK == 0

    def kv_spec(proj):
        return pl.BlockSpec((seq, HEAD_DIM), lambda b, g, i: (b, proj * KV_GROUPS + g))

    return pl.pallas_call(
        functools.partial(_nsa_prompt_body, tq=tq, tk=tk, seq=seq, n_cmp=n_cmp, hpg=hpg),
        out_shape=_sds((m, hd), BF16),
        grid=(batch, KV_GROUPS, nq),
        in_specs=[pl.BlockSpec(memory_space=pltpu.SMEM),
                  pl.BlockSpec((tq, gw), lambda b, g, i: (b * nq + i, g)),
                  pl.BlockSpec((tq, LANES), lambda b, g, i: (b * nq + i, g)),
                  pl.BlockSpec((None, ncp, HEAD_DIM), lambda b, g, i: (b, 0, g)),
                  pl.BlockSpec((None, ncp, HEAD_DIM), lambda b, g, i: (b, 0, g)),
                  kv_spec(2), kv_spec(3), kv_spec(4), kv_spec(5)],
        out_specs=pl.BlockSpec((tq, gw), lambda b, g, i: (b * nq + i, g)),
        scratch_shapes=[pltpu.VMEM((HEAD_DIM, seq), BF16),
                        pltpu.VMEM((HEAD_DIM, seq), BF16),
                        pltpu.VMEM((HEAD_DIM, ncp), BF16),
                        pltpu.VMEM((hpg, tk, tq), F32),
                        pltpu.VMEM((HEAD_DIM, hpg * tq), BF16),
                        pltpu.VMEM((nselp, tq), F32),
                        pltpu.VMEM((1, hpg * tq), F32),
                        pltpu.VMEM((1, hpg * tq), F32),
                        pltpu.VMEM((HEAD_DIM, hpg * tq), F32),
                        pltpu.VMEM((HEAD_DIM, hpg * tq), F32),
                        pltpu.VMEM((HEAD_DIM, hpg * tq), F32)],
        compiler_params=_cparams(("parallel", "parallel", "arbitrary")),
        name="nsa_prompt",
    )(slopes, q, gate, kc, vc, kvb, kvb, kvb, kvb)


def _stack_heads(q, g, hpg):
    return jnp.concatenate(
        [q[:, (g * hpg + h) * HEAD_DIM:(g * hpg + h + 1) * HEAD_DIM] for h in range(hpg)],
        axis=0).astype(BF16)


def _nsa_sample_cmp_win_body(q_ref, kc_ref, vc_ref, cwk_ref, cwv_ref, nwk_ref, nwv_ref,
                             ocmp_ref, owin_ref, sel_ref, *, s, past, n_cmp, n_sel, hpg):
    q = q_ref[...].astype(F32)
    ncp = kc_ref.shape[0]
    nbp = sel_ref.shape[-1]
    wbuf = cwk_ref.shape[0]
    n_pick = min(N_SELECT, n_sel)
    pos = past + _iota((s, 1), 0)
    cover = _cover(ncp, nbp)
    n_row = _iota((1, ncp), 1)
    dist_c = pos - (n_row * CMP_STRIDE + (CMP_BLOCK - 1))
    valid_c = (dist_c >= 0) & (n_row < n_cmp)
    j_row = _iota((1, nbp), 1)
    cur = pos // SEL_BLOCK
    forced = (j_row < N_INIT_BLOCKS) | ((cur - j_row >= 0) & (cur - j_row < N_LOCAL_BLOCKS))
    pad_rows = LANES - s
    d_w1 = pos - (past - wbuf + _iota((1, wbuf), 1))
    ok_w1 = (d_w1 >= 0) & (d_w1 < WINDOW) & (past - wbuf + _iota((1, wbuf), 1) >= 0)
    d_w2 = pos - (past + _iota((1, LANES), 1))
    ok_w2 = (d_w2 >= 0) & (d_w2 < WINDOW) & (_iota((1, LANES), 1) < s)

    for g in range(KV_GROUPS):
        gs = slice(g * HEAD_DIM, (g + 1) * HEAD_DIM)
        qs = _stack_heads(q, g, hpg)
        slope3 = _head_slopes(g, hpg, (hpg, 1, 1), 0)

        s3 = _dot_nt(qs, kc_ref[:, gs]).reshape(hpg, s, ncp)
        s3 = jnp.where(valid_c[None], s3 - slope3 * dist_c.astype(F32)[None], NEG)
        e = jnp.exp2(s3 - jnp.max(s3, axis=-1, keepdims=True))
        p3 = e / jnp.sum(e, axis=-1, keepdims=True) * valid_c.astype(F32)[None]
        ocmp_ref[g] = jnp.dot(p3.reshape(hpg * s, ncp).astype(BF16), vc_ref[:, gs],
                              preferred_element_type=F32)

        score = _split3_dot(jnp.sum(p3, axis=0), cover)
        score = jnp.where(forced, FORCE, score)
        score = jnp.where(j_row * SEL_BLOCK <= pos, score, NEG)
        score = jnp.where(j_row < n_sel, score, -jnp.inf)
        sel = (_select_rank(score, j_row, n_sel) < n_pick) & (j_row < n_sel)
        sel_ref[g] = sel.astype(F32)

        zpad = jnp.zeros((pad_rows, HEAD_DIM), F32)
        k2 = jnp.concatenate([nwk_ref[:, gs], zpad], axis=0).astype(BF16)
        v2 = jnp.concatenate([nwv_ref[:, gs], zpad], axis=0).astype(BF16)
        s1 = _dot_nt(qs, cwk_ref[:, gs].astype(BF16)).reshape(hpg, s, wbuf)
        s2 = _dot_nt(qs, k2).reshape(hpg, s, LANES)
        s1 = jnp.where(ok_w1[None], s1 - slope3 * d_w1.astype(F32)[None], NEG)
        s2 = jnp.where(ok_w2[None], s2 - slope3 * d_w2.astype(F32)[None], NEG)
        mx = jnp.maximum(jnp.max(s1, axis=-1, keepdims=True), jnp.max(s2, axis=-1, keepdims=True))
        p1 = jnp.exp2(s1 - mx)
        p2 = jnp.exp2(s2 - mx)
        den = jnp.sum(p1, axis=-1, keepdims=True) + jnp.sum(p2, axis=-1, keepdims=True)
        ow = (jnp.dot(p1.reshape(hpg * s, wbuf).astype(BF16), cwv_ref[:, gs].astype(BF16),
                      preferred_element_type=F32)
              + jnp.dot(p2.reshape(hpg * s, LANES).astype(BF16), v2, preferred_element_type=F32))
        owin_ref[g] = ow / den.reshape(hpg * s, 1)


def nsa_sample_cmp_win(q, kc, vc, cwk, cwv, kv_new, past, n_cmp, n_sel):
    b, s, hd = q.shape
    hpg = hd // HEAD_DIM // KV_GROUPS
    ncp = kc.shape[1]
    wbuf = cwk.shape[1]
    gd = KV_GROUPS * HEAD_DIM
    nbp = -(-n_sel // LANES) * LANES
    o_shape = _sds((b, KV_GROUPS, hpg * s, HEAD_DIM), F32)
    o_spec = pl.BlockSpec((None, KV_GROUPS, hpg * s, HEAD_DIM), lambda i: (i, 0, 0, 0))
    return pl.pallas_call(
        functools.partial(_nsa_sample_cmp_win_body, s=s, past=past, n_cmp=n_cmp, n_sel=n_sel, hpg=hpg),
        out_shape=[o_shape, o_shape, _sds((b, KV_GROUPS, s, nbp), F32)],
        grid=(b,),
        in_specs=[pl.BlockSpec((None, s, hd), lambda i: (i, 0, 0)),
                  pl.BlockSpec((None, ncp, gd), lambda i: (i, 0, 0)),
                  pl.BlockSpec((None, ncp, gd), lambda i: (i, 0, 0)),
                  pl.BlockSpec((None, wbuf, gd), lambda i: (i, 0, 0)),
                  pl.BlockSpec((None, wbuf, gd), lambda i: (i, 0, 0)),
                  pl.BlockSpec((None, s, gd), lambda i: (i, 0, 4)),
                  pl.BlockSpec((None, s, gd), lambda i: (i, 0, 5))],
        out_specs=[o_spec, o_spec,
                   pl.BlockSpec((None, KV_GROUPS, s, nbp), lambda i: (i, 0, 0, 0))],
        compiler_params=_cparams(("parallel",)),
        name="nsa_sample_cmp_win",
    )(q, kc, vc, cwk, cwv, kv_new, kv_new)


def _nsa_sample_slc_body(*refs, s, past, hpg, pps, n_chunks):
    ids_ref = refs[0]
    del ids_ref
    kp_refs = refs[1:1 + pps]
    vp_refs = refs[1 + pps:1 + 2 * pps]
    (q_ref, sel_ref, nk_ref, nv_ref, gate_ref, ocmp_ref, owin_ref, o_ref,
     qs_ref, m_ref, l_ref, acc_ref) = refs[1 + 2 * pps:]
    c = pl.program_id(1)
    nbp = sel_ref.shape[-1]
    tk = pps * PAGE_SIZE
    pos = past + _iota((s, 1), 0)

    @pl.when(c == 0)
    def _():
        q = q_ref[...].astype(F32)
        for g in range(KV_GROUPS):
            qs_ref[g] = _stack_heads(q, g, hpg)
            _flash_init(m_ref.at[g], l_ref.at[g], acc_ref.at[g])

    k0 = c * tk
    d = pos - (k0 + _iota((1, tk), 1))
    blk = (k0 + _iota((nbp, tk), 1)) // SEL_BLOCK
    expand = (_iota((nbp, tk), 0) == blk).astype(BF16)
    for g in range(KV_GROUPS):
        gs = slice(g * HEAD_DIM, (g + 1) * HEAD_DIM)
        slope3 = _head_slopes(g, hpg, (hpg, 1, 1), 0)
        grp = pl.ds(g, PAGE_SIZE, stride=KV_GROUPS)
        kk = jnp.concatenate([r[grp, :] for r in kp_refs], axis=0).astype(BF16)
        vv = jnp.concatenate([r[grp, :] for r in vp_refs], axis=0).astype(BF16)
        picked = jnp.dot(sel_ref[g].astype(BF16), expand, preferred_element_type=F32)
        ok = (picked > 0.5) & (d >= 0)
        sc = _dot_nt(qs_ref[g], kk).reshape(hpg, s, tk)
        sc = jnp.where(ok[None], sc - slope3 * d.astype(F32)[None], NEG)
        _flash_step(sc, vv, m_ref.at[g], l_ref.at[g], acc_ref.at[g])

    @pl.when(c == n_chunks - 1)
    def _():
        jn = past // SEL_BLOCK
        d2 = pos - (past + _iota((1, LANES), 1))
        in_new = (d2 >= 0) & (_iota((1, LANES), 1) < s)
        zpad = jnp.zeros((LANES - s, HEAD_DIM), F32)
        gate = gate_ref[...]
        for g in range(KV_GROUPS):
            gs = slice(g * HEAD_DIM, (g + 1) * HEAD_DIM)
            slope3 = _head_slopes(g, hpg, (hpg, 1, 1), 0)
            k2 = jnp.concatenate([nk_ref[:, gs], zpad], axis=0).astype(BF16)
            v2 = jnp.concatenate([nv_ref[:, gs], zpad], axis=0).astype(BF16)
            ok = in_new & (sel_ref[g][:, jn:jn + 1] > 0.5)
            sc = _dot_nt(qs_ref[g], k2).reshape(hpg, s, LANES)
            sc = jnp.where(ok[None], sc - slope3 * d2.astype(F32)[None], NEG)
            _flash_step(sc, v2, m_ref.at[g], l_ref.at[g], acc_ref.at[g])
            o_slc = acc_ref[g] / l_ref[g]
            o_cmp = ocmp_ref[g].reshape(hpg, s, HEAD_DIM)
            o_win = owin_ref[g].reshape(hpg, s, HEAD_DIM)
            for h in range(hpg):
                def gcol(br):
                    col = g * LANES + br * hpg + h
                    return gate[:, col:col + 1]
                o = gcol(0) * o_cmp[h] + gcol(1) * o_slc[h] + gcol(2) * o_win[h]
                hs = (g * hpg + h) * HEAD_DIM
                o_ref[:, hs:hs + HEAD_DIM] = o


def nsa_sample_slc(q, sel, kv_new, gate, o_cmp, o_win, pool_k, pool_v, page_table, past,
                   pages_per_step=4):
    b, s, hd = q.shape
    hpg = hd // HEAD_DIM // KV_GROUPS
    gd = KV_GROUPS * HEAD_DIM
    npages = page_table.shape[1]
    pps = pages_per_step
    assert npages % pps == 0 and s <= SEL_BLOCK and past % SEL_BLOCK == 0
    n_chunks = npages // pps
    nbp = sel.shape[-1]

    def page_spec(p):
        return pl.BlockSpec((None, PAGE_SIZE * KV_GROUPS, HEAD_DIM),
                            lambda i, c, ids: (ids[i * npages + c * pps + p], 0, 0))

    def per_b(shape):
        nd = len(shape)
        return pl.BlockSpec((None,) + shape, lambda i, c, ids: (i,) + (0,) * nd)

    o4 = (KV_GROUPS, hpg * s, HEAD_DIM)
    grid_spec = pltpu.PrefetchScalarGridSpec(
        num_scalar_prefetch=1,
        grid=(b, n_chunks),
        in_specs=[page_spec(p) for p in range(pps)] + [page_spec(p) for p in range(pps)] + [
            per_b((s, hd)), per_b((KV_GROUPS, s, nbp)),
            pl.BlockSpec((None, s, gd), lambda i, c, ids: (i, 0, 2)),
            pl.BlockSpec((None, s, gd), lambda i, c, ids: (i, 0, 3)),
            per_b((s, KV_GROUPS * LANES)), per_b(o4), per_b(o4)],
        out_specs=per_b((s, hd)),
        scratch_shapes=[pltpu.VMEM((KV_GROUPS, hpg * s, HEAD_DIM), BF16),
                        pltpu.VMEM((KV_GROUPS, hpg, s, 1), F32),
                        pltpu.VMEM((KV_GROUPS, hpg, s, 1), F32),
                        pltpu.VMEM((KV_GROUPS, hpg, s, HEAD_DIM), F32)],
    )
    return pl.pallas_call(
        functools.partial(_nsa_sample_slc_body, s=s, past=past, hpg=hpg, pps=pps, n_chunks=n_chunks),
        out_shape=_sds((b, s, hd), F32),
        grid_spec=grid_spec,
        compiler_params=_cparams(("parallel", "arbitrary")),
        name="nsa_sample_slc",
    )(page_table.reshape(-1), *([pool_k] * pps), *([pool_v] * pps),
      q, sel, kv_new, kv_new, gate, o_cmp, o_win)


def _gate_weight(w_qg, hd, hpg):
    wg = w_qg[:, hd:].reshape(-1, N_BRANCH, KV_GROUPS, hpg).transpose(0, 2, 1, 3)
    wg = wg.reshape(-1, KV_GROUPS, N_BRANCH * hpg)
    wg = jnp.pad(wg, ((0, 0), (0, 0), (0, LANES - N_BRANCH * hpg)))
    return wg.reshape(-1, KV_GROUPS * LANES).astype(BF16)


def kernel(x_prompt, x_sample, state_conv_a, state_ffn_conv, cache_cmp_k, cache_cmp_v, cache_slc_k,
           cache_slc_v, cache_win_k, cache_win_v, page_table, g_attn, g_ffn, g_final, w_pw1, w_dw_a,
           ln_a_g, ln_a_b, w_pw2, g_kv, w_kv, cmp_pe_k, cmp_w1_k, cmp_w2_k, cmp_pe_v, cmp_w1_v,
           cmp_w2_v, w_qg, w_o, w_up, w_dw_f, w_down):
    bp, tp, d = x_prompt.shape
    bs, ts, _ = x_sample.shape
    depth = g_attn.shape[0]
    n_a = w_pw1.shape[0]
    assert depth == n_a + 1, "one NSA layer after the convolution layers"
    f = w_down.shape[1]
    hd = w_o.shape[1]
    hpg = hd // HEAD_DIM // KV_GROUPS
    gd = KV_GROUPS * HEAD_DIM
    n_pages = page_table.shape[1]
    past = n_pages * PAGE_SIZE
    scale = HEAD_DIM ** -0.5
    mp, ms = bp * tp, bs * ts
    kconv = w_dw_a.shape[1]
    kffn = w_dw_f.shape[1]

    xp = x_prompt.reshape(mp, d)
    xs = x_sample.reshape(ms, d)
    conv_p, conv_s, ffn_p, ffn_s = [], [], [], []

    w_down_b = w_down.astype(BF16)
    w_up_b = w_up.astype(BF16)

    def conv_ffn_both(xp, xs, layer):
        hp = rmsnorm(xp, g_ffn[layer], BF16)
        hs = rmsnorm(xs, g_ffn[layer], BF16)
        act, sta, stb = ffn_up_rows(hp, tp, w_up_b, layer, w_dw_f)
        ffn_p.append(jnp.concatenate([sta, stb], axis=-1)[:, STATE_ROWS - (kffn - 1):])
        xp = mm(act, w_down_b, layer=layer, bm=512, bn=512, resid=xp, vmem_mib=60, name="ffn_down")
        up_s = mm(hs, w_up_b, layer=layer, bm=ms, bn=512, name="ffn_up_sample").reshape(bs, ts, 2 * f)
        xx = jnp.concatenate([state_ffn_conv[layer], up_s], axis=1)
        ffn_s.append(xx[:, -(kffn - 1):])
        act_s = ffn_gate_sample(xx, ts, w_dw_f[layer]).reshape(ms, f).astype(BF16)
        xs = mm(act_s, w_down_b, layer=layer, bm=ms, bn=512, resid=xs, vmem_mib=60, name="ffn_down_sample")
        return xp, xs

    for a in range(n_a):
        hp = rmsnorm(xp, g_attn[a], BF16)
        hs = rmsnorm(xs, g_attn[a], BF16)
        up, us = mm2(hp, hs, w_pw1, layer=a, bn=256, glu=True, name="pw1_glu")
        conv_p.append(up.reshape(bp, tp, d)[:, -(kconv - 1):])
        cp = conv_ln_prompt(up, tp, w_dw_a[a], ln_a_g[a], ln_a_b[a])
        xx = jnp.concatenate([state_conv_a[a], us.reshape(bs, ts, d)], axis=1)
        conv_s.append(xx[:, -(kconv - 1):])
        cs = conv_ln_sample(xx, ts, w_dw_a[a], ln_a_g[a], ln_a_b[a]).reshape(ms, d).astype(BF16)
        xp, xs = mm2(cp, cs, w_pw2, layer=a, resid=(xp, xs), name="pw2")

        xp, xs = conv_ffn_both(xp, xs, a)

    wkv = w_kv.astype(BF16)
    kv_rows_p, kvb_p = kv_proj(rmsnorm(xp, g_kv, BF16), wkv)
    kv_p4 = [r.reshape(bp, tp, KV_GROUPS, HEAD_DIM) for r in kv_rows_p]
    kv_s = mm(rmsnorm(xs, g_kv, BF16), wkv, bm=ms, bn=512, name="kv_sample")
    kv_s5 = kv_s.reshape(bs, ts, 2 * N_BRANCH, KV_GROUPS, HEAD_DIM)
    kv_s3 = kv_s.reshape(bs, ts, 2 * N_BRANCH * gd)

    half_w = CMP_STRIDE * gd
    nhp = tp // CMP_STRIDE
    n_cmp_p = nhp - 1
    assert tp % CMP_STRIDE == 0 and ts <= CMP_STRIDE and past % PAGE_SIZE == 0
    nhs = past // CMP_STRIDE
    n_cmp_s = nhs
    page_ids = page_table.reshape(-1)
    kcs = []
    for proj, (pe, w1c, w2c, pool) in enumerate([(cmp_pe_k, cmp_w1_k, cmp_w2_k, cache_cmp_k),
                                                 (cmp_pe_v, cmp_w1_v, cmp_w2_v, cache_cmp_v)]):
        pebig, wbig = _cmp_weights(pe, w1c)
        w2b = w2c.astype(BF16)
        rows_per_page = PAGE_SIZE * KV_GROUPS
        lo, hi = cmp_proj_pool(kv_rows_p[proj].reshape(-1, rows_per_page, HEAD_DIM),
                               jnp.arange(mp // PAGE_SIZE, dtype=jnp.int32), pebig, wbig,
                               pages_per_step=tp // PAGE_SIZE)
        lo = lo.reshape(bp, nhp, gd)
        hi = hi.reshape(bp, nhp, gd)
        hi_next = jnp.concatenate([hi[:, 1:], jnp.zeros((bp, 1, gd), F32)], axis=1)
        _, c_p = cmp_out(lo, hi_next, w2b)

        lo_h, hi_h = cmp_proj_pool(pool.reshape(pool.shape[0], PAGE_SIZE * KV_GROUPS, HEAD_DIM),
                                   page_ids, pebig, wbig, pages_per_step=min(32, bs * n_pages))
        new_rows = jnp.pad(kv_s3[:, :, proj * gd:(proj + 1) * gd], ((0, 0), (0, CMP_STRIDE - ts), (0, 0)))
        new_halves = new_rows.reshape(bs, CMP_STRIDE, KV_GROUPS, HEAD_DIM).transpose(2, 0, 1, 3)
        _, hi_n = cmp_proj_halves(new_halves.reshape(KV_GROUPS, bs, CMP_STRIDE * HEAD_DIM), pebig, wbig)
        hi_next_s = jnp.concatenate([hi_h.reshape(bs, nhs, gd)[:, 1:], hi_n[:, None]], axis=1)
        _, c_s = cmp_out(lo_h.reshape(bs, nhs, gd), hi_next_s, w2b)
        kcs.append((c_p, c_s))
    (kc_p, kc_s), (vc_p, vc_s) = kcs

    b_l = n_a
    wg = _gate_weight(w_qg[0], hd, hpg)
    hp = rmsnorm(xp, g_attn[b_l], BF16)
    hs = rmsnorm(xs, g_attn[b_l], BF16)
    q_p, q_s = mm2(hp, hs, w_qg, layer=0, n=hd, scale=scale * LOG2E, out_dtype=BF16, name="q_proj")
    gate_p, gate_s = mm2(hp, hs, wg, act="sigmoid", name="gate_proj")
    n_heads = hd // HEAD_DIM
    slopes = jnp.exp2(-8.0 * jnp.arange(1, n_heads + 1, dtype=F32) / n_heads) * LOG2E
    o_p = nsa_prompt(q_p, gate_p, kc_p, vc_p, kvb_p, slopes, bp, tp, n_cmp_p)

    q_s3 = q_s.reshape(bs, ts, hd)
    n_sel_s = past // SEL_BLOCK + -(-ts // SEL_BLOCK)
    wbuf = cache_win_k.shape[1]
    o_cmp_s, o_win_s, sel_s = nsa_sample_cmp_win(
        q_s3, kc_s, vc_s, cache_win_k.reshape(bs, wbuf, gd), cache_win_v.reshape(bs, wbuf, gd),
        kv_s3, past, n_cmp_s, n_sel_s)
    o_s = nsa_sample_slc(q_s3, sel_s, kv_s3, gate_s.reshape(bs, ts, KV_GROUPS * LANES), o_cmp_s, o_win_s,
                         cache_slc_k.reshape(-1, PAGE_SIZE * KV_GROUPS, HEAD_DIM),
                         cache_slc_v.reshape(-1, PAGE_SIZE * KV_GROUPS, HEAD_DIM),
                         page_table, past, pages_per_step=min(16, n_pages))
    xp, xs = mm2(o_p, o_s.reshape(ms, hd).astype(BF16), w_o, layer=0, resid=(xp, xs), name="o_proj")

    xp, xs = conv_ffn_both(xp, xs, b_l)

    y_prompt = rmsnorm(xp, g_final, F32).reshape(bp, tp, d)
    y_sample = rmsnorm(xs, g_final, F32).reshape(bs, ts, d)
    keep = min(WINDOW, tp)
    win_k_s = jnp.concatenate([cache_win_k, kv_s5[:, :, 4]], axis=1)[:, -wbuf:]
    win_v_s = jnp.concatenate([cache_win_v, kv_s5[:, :, 5]], axis=1)[:, -wbuf:]
    return (y_prompt, y_sample, jnp.stack(conv_p), jnp.stack(conv_s), jnp.stack(ffn_p), jnp.stack(ffn_s),
            kv_p4[0], kv_p4[1], kv_p4[2], kv_p4[3],
            kv_s5[:, :, 0], kv_s5[:, :, 1], kv_s5[:, :, 2], kv_s5[:, :, 3],
            kv_p4[4][:, -keep:], kv_p4[5][:, -keep:], win_k_s, win_v_s)
```

```python
import functools

import jax
import jax.numpy as jnp
from jax import lax
from jax.experimental import pallas as pl
from jax.experimental.pallas import tpu as pltpu

F32 = jnp.float32
BF16 = jnp.bfloat16

HEAD_DIM = 128
KV_GROUPS = 4
N_BRANCH = 3
CMP_BLOCK = 32
CMP_STRIDE = 16
SEL_BLOCK = 64
N_SELECT = 16
N_INIT_BLOCKS = 1
N_LOCAL_BLOCKS = 2
WINDOW = 512
PAGE_SIZE = 128
EPS = 1e-6
NEG = -1e30
FORCE = 1e9

LOG2E = 1.4426950408889634
LANES = 128
BF16_ROWS = 16
MIB = 1024 * 1024


def _cparams(sem, vmem_mib=48):
    return pltpu.CompilerParams(dimension_semantics=sem, vmem_limit_bytes=vmem_mib * MIB)


def _sds(shape, dtype):
    return jax.ShapeDtypeStruct(shape, dtype)


def _silu(x):
    return x * jax.nn.sigmoid(x)


def _rmsnorm_body(x_ref, g_ref, o_ref):
    x = x_ref[...]
    y = x * lax.rsqrt(jnp.mean(x * x, axis=-1, keepdims=True) + EPS)
    o_ref[...] = (y * g_ref[...]).astype(o_ref.dtype)


def rmsnorm(x, g, out_dtype, bm=512):
    m, d = x.shape
    bm = min(bm, m)
    return pl.pallas_call(
        _rmsnorm_body,
        out_shape=_sds((m, d), out_dtype),
        grid=(m // bm,),
        in_specs=[pl.BlockSpec((bm, d), lambda i: (i, 0)),
                  pl.BlockSpec((1, d), lambda i: (0, 0))],
        out_specs=pl.BlockSpec((bm, d), lambda i: (i, 0)),
        compiler_params=_cparams(("parallel",)),
        name="rmsnorm",
    )(x, g.reshape(1, d))


def _mm_body(*refs, glu, has_resid, scale, act, n_out):
    x_ref = refs[0]
    pos = 1
    x = x_ref[...]
    acc = jnp.dot(x, refs[pos][...], preferred_element_type=F32)
    pos += 1
    if glu:
        gate = jnp.dot(x, refs[pos][...], preferred_element_type=F32)
        pos += 1
        acc = acc * jax.nn.sigmoid(gate)
    if scale is not None:
        acc = acc * scale
    if act == "sigmoid":
        acc = jax.nn.sigmoid(acc)
    if has_resid:
        acc = acc + refs[pos][...]
        pos += 1
    for o_ref in refs[pos:pos + n_out]:
        o_ref[...] = acc.astype(o_ref.dtype)


def mm(x, w, *, bm, bn, layer=None, glu=False, resid=None, scale=None, act=None,
       out_dtypes=(F32,), vmem_mib=48, name="mm"):
    m, k = x.shape
    n = w.shape[-1] // 2 if glu else w.shape[-1]
    bm = min(bm, m)
    bn = min(bn, n)
    assert m % bm == 0 and n % bn == 0
    nb = n // bn
    if w.ndim == 3:
        def w_spec(off):
            return pl.BlockSpec((None, k, bn), lambda i, j: (layer, 0, j + off))
    else:
        def w_spec(off):
            return pl.BlockSpec((k, bn), lambda i, j: (0, j + off))
    in_specs = [pl.BlockSpec((bm, k), lambda i, j: (i, 0)), w_spec(0)]
    args = [x, w]
    if glu:
        in_specs.append(w_spec(nb))
        args.append(w)
    if resid is not None:
        in_specs.append(pl.BlockSpec((bm, bn), lambda i, j: (i, j)))
        args.append(resid)
    outs = pl.pallas_call(
        functools.partial(_mm_body, glu=glu, has_resid=resid is not None, scale=scale,
                          act=act, n_out=len(out_dtypes)),
        out_shape=[_sds((m, n), dt) for dt in out_dtypes],
        grid=(m // bm, n // bn),
        in_specs=in_specs,
        out_specs=[pl.BlockSpec((bm, bn), lambda i, j: (i, j)) for _ in out_dtypes],
        compiler_params=_cparams(("parallel", "arbitrary"), vmem_mib),
        name=name,
    )(*args)
    return outs[0] if len(out_dtypes) == 1 else outs


def _mm2_body(*refs, glu, has_resid, scale, act, ni):
    it = iter(refs)
    xp_ref, xs_ref, wa_ref = next(it), next(it), next(it)
    wb_ref = next(it) if glu else None
    rp_ref, rs_ref = (next(it), next(it)) if has_resid else (None, None)
    yp_ref, ys_ref, wsa_ref = next(it), next(it), next(it)
    wsb_ref = next(it) if glu else None
    i = pl.program_id(1)

    @pl.when(i == 0)
    def _():
        wsa_ref[...] = wa_ref[...].astype(BF16)
        if glu:
            wsb_ref[...] = wb_ref[...].astype(BF16)

    def compute(x_ref, r_ref, y_ref):
        x = x_ref[...]
        acc = jnp.dot(x, wsa_ref[...], preferred_element_type=F32)
        if glu:
            acc = acc * jax.nn.sigmoid(jnp.dot(x, wsb_ref[...], preferred_element_type=F32))
        if scale is not None:
            acc = acc * scale
        if act == "sigmoid":
            acc = jax.nn.sigmoid(acc)
        if r_ref is not None:
            acc = acc + r_ref[...]
        y_ref[...] = acc.astype(y_ref.dtype)

    @pl.when(i < ni)
    def _():
        compute(xp_ref, rp_ref, yp_ref)

    @pl.when(i == ni)
    def _():
        compute(xs_ref, rs_ref, ys_ref)


def mm2(xp, xs, w, *, n=None, layer=None, bm=1024, bn=512, glu=False, resid=None, scale=None, act=None,
        out_dtype=F32, vmem_mib=56, name="mm2"):
    mp, k = xp.shape
    ms = xs.shape[0]
    nw = w.shape[-1]
    if n is None:
        n = nw // 2 if glu else nw
    bm = min(bm, mp)
    bn = min(bn, n)
    assert mp % bm == 0 and n % bn == 0
    ni, nb = mp // bm, n // bn
    if w.ndim == 3:
        def w_spec(off):
            return pl.BlockSpec((None, k, bn), lambda j, i: (layer, 0, j + off))
    else:
        def w_spec(off):
            return pl.BlockSpec((k, bn), lambda j, i: (0, j + off))
    prow = lambda j, i: (jnp.minimum(i, ni - 1), 0)
    pblk = lambda j, i: (jnp.minimum(i, ni - 1), j)
    in_specs = [pl.BlockSpec((bm, k), prow), pl.BlockSpec((ms, k), lambda j, i: (0, 0)), w_spec(0)]
    args = [xp, xs, w]
    if glu:
        in_specs.append(w_spec(nb))
        args.append(w)
    if resid is not None:
        in_specs += [pl.BlockSpec((bm, bn), pblk), pl.BlockSpec((ms, bn), lambda j, i: (0, j))]
        args += list(resid)
    scratch = [pltpu.VMEM((k, bn), BF16)] * (2 if glu else 1)
    return pl.pallas_call(
        functools.partial(_mm2_body, glu=glu, has_resid=resid is not None, scale=scale, act=act, ni=ni),
        out_shape=[_sds((mp, n), out_dtype), _sds((ms, n), out_dtype)],
        grid=(nb, ni + 1),
        in_specs=in_specs,
        out_specs=[pl.BlockSpec((bm, bn), pblk), pl.BlockSpec((ms, bn), lambda j, i: (0, j))],
        scratch_shapes=scratch,
        compiler_params=_cparams(("parallel", "arbitrary"), vmem_mib),
        name=name,
    )(*args)


def _kv_proj_body(x_ref, w_ref, *o_refs, n_proj):
    j = pl.program_id(1)
    bm = x_ref.shape[0]
    acc = jnp.dot(x_ref[...], w_ref[...], preferred_element_type=F32)
    o_refs[n_proj][...] = acc.astype(BF16)
    for p in range(n_proj):
        @pl.when(j == p)
        def _(p=p):
            for g in range(KV_GROUPS):
                o_refs[p][pl.ds(g, bm, stride=KV_GROUPS), :] = acc[:, g * HEAD_DIM:(g + 1) * HEAD_DIM]


def kv_proj(x, w, bm=512):
    m, k = x.shape
    gd = KV_GROUPS * HEAD_DIM
    n_proj = w.shape[1] // gd
    assert m % bm == 0
    row_spec = pl.BlockSpec((bm * KV_GROUPS, HEAD_DIM), lambda i, j: (i, 0))
    outs = pl.pallas_call(
        functools.partial(_kv_proj_body, n_proj=n_proj),
        out_shape=[_sds((m * KV_GROUPS, HEAD_DIM), F32)] * n_proj + [_sds((m, n_proj * gd), BF16)],
        grid=(m // bm, n_proj),
        in_specs=[pl.BlockSpec((bm, k), lambda i, j: (i, 0)),
                  pl.BlockSpec((k, gd), lambda i, j: (0, j))],
        out_specs=[row_spec] * n_proj + [pl.BlockSpec((bm, gd), lambda i, j: (i, j))],
        compiler_params=_cparams(("parallel", "arbitrary")),
        name="kv_proj",
    )(x, w)
    return outs[:n_proj], outs[n_proj]


def _ln_silu(c, g, b):
    mu = jnp.mean(c, axis=-1, keepdims=True)
    xc = c - mu
    y = xc * lax.rsqrt(jnp.mean(xc * xc, axis=-1, keepdims=True) + EPS)
    return _silu(y * g + b)


CONV_HALO = 32


SUBLANES = 8


def _conv_ln_body(u_ref, halo_ref, w_ref, g_ref, b_ref, o_ref, xs_ref, sh_ref, c_ref,
                  *, bt, blocks_per_seq, kw, cw, rc):
    i = pl.program_id(0)
    d = u_ref.shape[1]
    keep = jnp.where(i % blocks_per_seq == 0, 0.0, 1.0)
    xs_ref[0:CONV_HALO, :] = halo_ref[...] * keep
    xs_ref[CONV_HALO:, :] = u_ref[...]
    base = CONV_HALO - (kw - 1)
    span = sh_ref.shape[1]

    def chunk(ci, carry):
        lanes = pl.ds(pl.multiple_of(ci * cw, cw), cw)
        for r in range(1, SUBLANES):
            sh_ref[r - 1] = xs_ref[r:r + span, lanes]
        for rb in range(bt // rc):
            acc = jnp.zeros((rc, cw), F32)
            for k in range(kw):
                a, r = divmod(base + k, SUBLANES)
                row0 = rb * rc + SUBLANES * a
                if r == 0:
                    x = xs_ref[row0:row0 + rc, lanes]
                else:
                    x = sh_ref[r - 1, row0:row0 + rc, :]
                acc = acc + w_ref[k:k + 1, lanes] * x
            c_ref[rb * rc:(rb + 1) * rc, lanes] = acc
        return carry

    lax.fori_loop(0, d // cw, chunk, 0)

    def ln(rb, carry):
        rows = pl.ds(pl.multiple_of(rb * rc, rc), rc)
        o_ref[rows, :] = _ln_silu(c_ref[rows, :], g_ref[...], b_ref[...]).astype(o_ref.dtype)
        return carry

    lax.fori_loop(0, bt // rc, ln, 0)


def conv_ln_prompt(u, seq, w_dw, ln_g, ln_b, bt=256, cw=512, rc=64):
    m, d = u.shape
    kw = w_dw.shape[0]
    assert kw - 1 <= CONV_HALO and seq % bt == 0 and bt % CONV_HALO == 0
    assert d % cw == 0 and bt % rc == 0
    r = bt // CONV_HALO
    span = bt + CONV_HALO - SUBLANES
    return pl.pallas_call(
        functools.partial(_conv_ln_body, bt=bt, blocks_per_seq=seq // bt, kw=kw, cw=cw, rc=rc),
        out_shape=_sds((m, d), BF16),
        grid=(m // bt,),
        in_specs=[pl.BlockSpec((bt, d), lambda i: (i, 0)),
                  pl.BlockSpec((CONV_HALO, d), lambda i: (jnp.maximum(i * r - 1, 0), 0)),
                  pl.BlockSpec((kw, d), lambda i: (0, 0)),
                  pl.BlockSpec((1, d), lambda i: (0, 0)),
                  pl.BlockSpec((1, d), lambda i: (0, 0))],
        out_specs=pl.BlockSpec((bt, d), lambda i: (i, 0)),
        scratch_shapes=[pltpu.VMEM((bt + CONV_HALO, d), F32),
                        pltpu.VMEM((SUBLANES - 1, span, cw), F32),
                        pltpu.VMEM((bt, d), F32)],
        compiler_params=_cparams(("parallel",)),
        name="conv_ln_prompt",
    )(u, u, w_dw, ln_g.reshape(1, d), ln_b.reshape(1, d))


def _conv_ln_sample_body(xx_ref, w_ref, g_ref, b_ref, o_ref, *, s, kw):
    acc = jnp.zeros((s, xx_ref.shape[-1]), F32)
    for k in range(kw):
        acc = acc + w_ref[k:k + 1, :] * xx_ref[k:k + s, :]
    o_ref[...] = _ln_silu(acc, g_ref[...], b_ref[...])


def conv_ln_sample(xx, s, w_dw, ln_g, ln_b):
    b, l, d = xx.shape
    kw = w_dw.shape[0]
    return pl.pallas_call(
        functools.partial(_conv_ln_sample_body, s=s, kw=kw),
        out_shape=_sds((b, s, d), F32),
        grid=(b,),
        in_specs=[pl.BlockSpec((None, l, d), lambda i: (i, 0, 0)),
                  pl.BlockSpec((kw, d), lambda i: (0, 0)),
                  pl.BlockSpec((1, d), lambda i: (0, 0)),
                  pl.BlockSpec((1, d), lambda i: (0, 0))],
        out_specs=pl.BlockSpec((None, s, d), lambda i: (i, 0, 0)),
        compiler_params=_cparams(("parallel",)),
        name="conv_ln_sample",
    )(xx, w_dw, ln_g.reshape(1, d), ln_b.reshape(1, d))


FFN_HALO = BF16_ROWS
STATE_ROWS = 8


def _ffn_up_rows_body(h_ref, halo_ref, wa_ref, wb_ref, dwa_ref, dwb_ref, act_ref, sta_ref, stb_ref,
                      hx_ref, *, bm, blocks_per_seq, kw):
    i = pl.program_id(0)

    @pl.when(pl.program_id(1) == 0)
    def _():
        keep = jnp.where(i % blocks_per_seq == 0, 0.0, 1.0)
        hx_ref[0:FFN_HALO, :] = (halo_ref[...].astype(F32) * keep).astype(BF16)
        hx_ref[FFN_HALO:, :] = h_ref[...]

    hx = hx_ref[...]
    ua = jnp.dot(hx, wa_ref[...], preferred_element_type=F32)
    ub = jnp.dot(hx, wb_ref[...], preferred_element_type=F32)

    def conv(u, dw_ref):
        c = jnp.zeros((bm, u.shape[1]), F32)
        for k in range(kw):
            off = FFN_HALO - (kw - 1) + k
            c = c + dw_ref[k:k + 1, :] * u[off:off + bm, :]
        return c

    act_ref[...] = (_silu(conv(ua, dwa_ref)) * conv(ub, dwb_ref)).astype(act_ref.dtype)
    sta_ref[...] = ua[FFN_HALO + bm - STATE_ROWS:, :]
    stb_ref[...] = ub[FFN_HALO + bm - STATE_ROWS:, :]


def ffn_up_rows(h, seq, w_up, layer, w_dw, bm=1024, bn=256, vmem_mib=48):
    m, d = h.shape
    f = w_up.shape[-1] // 2
    kw = w_dw.shape[1]
    assert seq % bm == 0 and f % bn == 0 and kw - 1 <= FFN_HALO
    bps = seq // bm
    r = bm // FFN_HALO
    nb = f // bn
    act, sta, stb = pl.pallas_call(
        functools.partial(_ffn_up_rows_body, bm=bm, blocks_per_seq=bps, kw=kw),
        out_shape=[_sds((m, f), BF16), _sds((m // bm, STATE_ROWS, f), F32),
                   _sds((m // bm, STATE_ROWS, f), F32)],
        grid=(m // bm, nb),
        in_specs=[pl.BlockSpec((bm, d), lambda i, j: (i, 0)),
                  pl.BlockSpec((FFN_HALO, d), lambda i, j: (jnp.maximum(i * r - 1, 0), 0)),
                  pl.BlockSpec((None, d, bn), lambda i, j: (layer, 0, j)),
                  pl.BlockSpec((None, d, bn), lambda i, j: (layer, 0, j + nb)),
                  pl.BlockSpec((None, kw, bn), lambda i, j: (layer, 0, j)),
                  pl.BlockSpec((None, kw, bn), lambda i, j: (layer, 0, j + nb))],
        out_specs=[pl.BlockSpec((bm, bn), lambda i, j: (i, j)),
                   pl.BlockSpec((None, STATE_ROWS, bn), lambda i, j: (i, 0, j)),
                   pl.BlockSpec((None, STATE_ROWS, bn), lambda i, j: (i, 0, j))],
        scratch_shapes=[pltpu.VMEM((bm + FFN_HALO, d), BF16)],
        compiler_params=_cparams(("parallel", "arbitrary"), vmem_mib),
        name="ffn_up_rows",
    )(h, h, w_up, w_up, w_dw, w_dw)
    return act, sta[bps - 1::bps], stb[bps - 1::bps]


def _ffn_gate_sample_body(xa_ref, xb_ref, dwa_ref, dwb_ref, o_ref, *, s, kw):
    def conv(x_ref, dw_ref):
        c = jnp.zeros((x_ref.shape[0], s, x_ref.shape[2]), F32)
        for k in range(kw):
            c = c + dw_ref[k:k + 1, :] * x_ref[:, k:k + s, :]
        return c

    o_ref[...] = _silu(conv(xa_ref, dwa_ref)) * conv(xb_ref, dwb_ref)


def ffn_gate_sample(xx, s, w_dw):
    b, l, f2 = xx.shape
    f = f2 // 2
    kw = w_dw.shape[0]
    return pl.pallas_call(
        functools.partial(_ffn_gate_sample_body, s=s, kw=kw),
        out_shape=_sds((b, s, f), F32),
        grid=(1,),
        in_specs=[pl.BlockSpec((b, l, f), lambda i: (0, 0, 0)),
                  pl.BlockSpec((b, l, f), lambda i: (0, 0, 1)),
                  pl.BlockSpec((kw, f), lambda i: (0, 0)),
                  pl.BlockSpec((kw, f), lambda i: (0, 1))],
        out_specs=pl.BlockSpec((b, s, f), lambda i: (0, 0, 0)),
        compiler_params=_cparams(("arbitrary",)),
        name="ffn_gate_sample",
    )(xx, xx, w_dw, w_dw)


def _cmp_half_dots(xs_ref, pe_ref, w_ref, lo_ref, hi_ref):
    for g in range(KV_GROUPS):
        x = xs_ref[g]
        sl = slice(g * HEAD_DIM, (g + 1) * HEAD_DIM)
        lo_ref[:, sl] = jnp.dot((x + pe_ref[0:1, :]).astype(BF16), w_ref[0], preferred_element_type=F32)
        hi_ref[:, sl] = jnp.dot((x + pe_ref[1:2, :]).astype(BF16), w_ref[1], preferred_element_type=F32)


def cmp_proj_halves(xs, pe, w):
    _, r, kk = xs.shape
    n = w.shape[2]
    return pl.pallas_call(
        _cmp_half_dots,
        out_shape=[_sds((r, KV_GROUPS * n), F32)] * 2,
        grid=(1,),
        in_specs=[pl.BlockSpec((KV_GROUPS, r, kk), lambda i: (0, 0, 0)),
                  pl.BlockSpec((2, kk), lambda i: (0, 0)),
                  pl.BlockSpec((2, kk, n), lambda i: (0, 0, 0))],
        out_specs=[pl.BlockSpec((r, KV_GROUPS * n), lambda i: (0, 0))] * 2,
        compiler_params=_cparams(("arbitrary",)),
        name="cmp_proj_halves",
    )(xs, pe, w)


def _cmp_proj_pool_body(*refs, pps):
    pg_refs = refs[1:1 + pps]
    pe_ref, w_ref, lo_ref, hi_ref, xs_ref = refs[1 + pps:]
    hp = pg_refs[0].shape[0] // (CMP_STRIDE * KV_GROUPS)
    for p, pg_ref in enumerate(pg_refs):
        for j in range(CMP_STRIDE):
            for g in range(KV_GROUPS):
                rows = pl.ds(j * KV_GROUPS + g, hp, stride=CMP_STRIDE * KV_GROUPS)
                xs_ref[g, p * hp:(p + 1) * hp, j * HEAD_DIM:(j + 1) * HEAD_DIM] = pg_ref[rows, :]
    _cmp_half_dots(xs_ref, pe_ref, w_ref, lo_ref, hi_ref)


def cmp_proj_pool(pool, page_ids, pe, w, pages_per_step=32):
    _, pr, _ = pool.shape
    hp = pr // (CMP_STRIDE * KV_GROUPS)
    kk = CMP_STRIDE * HEAD_DIM
    n = w.shape[2]
    npg = page_ids.shape[0]
    pps = pages_per_step
    assert npg % pps == 0 and hp == SUBLANES
    bm = hp * pps

    def page_spec(p):
        return pl.BlockSpec((None, pr, HEAD_DIM), lambda i, ids: (ids[i * pps + p], 0, 0))

    grid_spec = pltpu.PrefetchScalarGridSpec(
        num_scalar_prefetch=1,
        grid=(npg // pps,),
        in_specs=[page_spec(p) for p in range(pps)] + [
            pl.BlockSpec((2, kk), lambda i, ids: (0, 0)),
            pl.BlockSpec((2, kk, n), lambda i, ids: (0, 0, 0))],
        out_specs=[pl.BlockSpec((bm, KV_GROUPS * n), lambda i, ids: (i, 0))] * 2,
        scratch_shapes=[pltpu.VMEM((KV_GROUPS, bm, kk), F32)],
    )
    return pl.pallas_call(
        functools.partial(_cmp_proj_pool_body, pps=pps),
        out_shape=[_sds((npg * hp, KV_GROUPS * n), F32)] * 2,
        grid_spec=grid_spec,
        compiler_params=_cparams(("arbitrary",)),
        name="cmp_proj_pool",
    )(page_ids, *([pool] * pps), pe, w)


def _cmp_out_body(lo_ref, hi_ref, w2_ref, o_ref, ob_ref):
    s = _silu(lo_ref[...] + hi_ref[...])
    for g in range(KV_GROUPS):
        sl = slice(g * HEAD_DIM, (g + 1) * HEAD_DIM)
        y = jnp.dot(s[:, sl].astype(BF16), w2_ref[...], preferred_element_type=F32)
        o_ref[:, sl] = y
        ob_ref[:, sl] = y.astype(BF16)


def cmp_out(lo, hi_next, w2, bn=128):
    b, nc, n = lo.shape
    bn = min(bn, nc)
    spec = pl.BlockSpec((None, bn, n), lambda i, j: (i, j, 0))
    return pl.pallas_call(
        _cmp_out_body,
        out_shape=[_sds((b, nc, n), F32), _sds((b, nc, n), BF16)],
        grid=(b, nc // bn),
        in_specs=[spec, spec, pl.BlockSpec((HEAD_DIM, HEAD_DIM), lambda i, j: (0, 0))],
        out_specs=[spec, spec],
        compiler_params=_cparams(("parallel", "parallel")),
        name="cmp_out",
    )(lo, hi_next, w2)


def _cmp_weights(pe, w1):
    half = CMP_STRIDE * HEAD_DIM
    return pe.reshape(2, half), w1.reshape(2, half, -1).astype(BF16)


def _iota(shape, dim):
    return lax.broadcasted_iota(jnp.int32, shape, dim)


def _dot_nt(a, b):
    return lax.dot_general(a, b, (((1,), (1,)), ((), ())), preferred_element_type=F32)


def _split3_dot(p, c):
    p1 = p.astype(BF16)
    r1 = p - p1.astype(F32)
    p2 = r1.astype(BF16)
    p3 = (r1 - p2.astype(F32)).astype(BF16)
    return (jnp.dot(p1, c, preferred_element_type=F32) + jnp.dot(p2, c, preferred_element_type=F32)
            + jnp.dot(p3, c, preferred_element_type=F32))


def _cover(n_cmp, n_sel):
    n = _iota((n_cmp, n_sel), 0)
    j = _iota((n_cmp, n_sel), 1)
    lo = jnp.maximum(n * CMP_STRIDE, j * SEL_BLOCK)
    hi = jnp.minimum(n * CMP_STRIDE + CMP_BLOCK, (j + 1) * SEL_BLOCK)
    return (jnp.maximum(hi - lo, 0).astype(F32) * (1.0 / CMP_STRIDE)).astype(BF16)


def _head_slopes(g, hpg, shape, dim):
    h = g * hpg + _iota(shape, dim) + 1
    return jnp.exp2(h.astype(F32) * (-8.0 / (KV_GROUPS * hpg))) * LOG2E


def _flash_step(s3, v, m_ref, l_ref, acc_ref):
    hh, tq, tk = s3.shape
    m_prev = m_ref[...]
    m_new = jnp.maximum(m_prev, jnp.max(s3, axis=-1, keepdims=True))
    alpha = jnp.exp2(m_prev - m_new)
    p = jnp.exp2(s3 - m_new)
    l_ref[...] = alpha * l_ref[...] + jnp.sum(p, axis=-1, keepdims=True)
    pv = jnp.dot(p.reshape(hh * tq, tk).astype(BF16), v, preferred_element_type=F32)
    acc_ref[...] = alpha * acc_ref[...] + pv.reshape(hh, tq, HEAD_DIM)
    m_ref[...] = m_new


def _flash_init(m_ref, l_ref, acc_ref):
    m_ref[...] = jnp.full(m_ref.shape, NEG, F32)
    l_ref[...] = jnp.zeros(l_ref.shape, F32)
    acc_ref[...] = jnp.zeros(acc_ref.shape, F32)


def _select_rank(score, idx, n_candidates):
    rank = jnp.zeros(score.shape, jnp.int32)
    for i in range(n_candidates):
        si = score[:, i:i + 1]
        before = (si > score) | ((si == score) & (i < idx))
        rank = rank + before.astype(jnp.int32)
    return rank


def _cover_t(n_sel, n_cmp):
    j = _iota((n_sel, n_cmp), 0)
    n = _iota((n_sel, n_cmp), 1)
    lo = jnp.maximum(n * CMP_STRIDE, j * SEL_BLOCK)
    hi = jnp.minimum(n * CMP_STRIDE + CMP_BLOCK, (j + 1) * SEL_BLOCK)
    return (jnp.maximum(hi - lo, 0).astype(F32) * (1.0 / CMP_STRIDE)).astype(BF16)


def _nsa_prompt_body(slopes_ref, q_ref, gate_ref, kc_ref, vc_ref, sk_ref, sv_ref, wk_ref, wv_ref, o_ref,
                     vts_ref, vtw_ref, vct_ref, sd_ref, qt_ref, selt_ref, m_ref, l_ref, acc_ref,
                     ocmp_ref, oslc_ref, *, tq, tk, seq, n_cmp, hpg):
    g = pl.program_id(1)
    i = pl.program_id(2)
    t0 = i * tq
    n_sel = seq // SEL_BLOCK
    n_pick = min(N_SELECT, n_sel)
    nselp = selt_ref.shape[0]
    ncp = kc_ref.shape[0]
    bpt = tk // SEL_BLOCK

    def slope(h):
        return slopes_ref[g * hpg + h]

    def cols(h):
        return slice(h * tq, (h + 1) * tq)

    d0 = _iota((tk, tq), 1) - _iota((tk, tq), 0)

    @pl.when(i == 0)
    def _():
        vts_ref[...] = sv_ref[...].astype(F32).T.astype(BF16)
        vtw_ref[...] = wv_ref[...].astype(F32).T.astype(BF16)
        vct_ref[...] = vc_ref[...].astype(F32).T.astype(BF16)
        d0f = d0.astype(F32)
        for h in range(hpg):
            sd_ref[h] = slope(h) * d0f

    for h in range(hpg):
        qt_ref[:, cols(h)] = q_ref[:, h * HEAD_DIM:(h + 1) * HEAD_DIM].astype(F32).T.astype(BF16)

    t_row = t0 + _iota((1, tq), 1)

    n_col = _iota((ncp, 1), 0)
    dist_c = t_row - (n_col * CMP_STRIDE + (CMP_BLOCK - 1))
    valid_c = (dist_c >= 0) & (n_col < n_cmp)
    dist_cf = dist_c.astype(F32)
    valid_cf = valid_c.astype(F32)
    s_all = jnp.dot(kc_ref[...], qt_ref[...], preferred_element_type=F32)
    psum = jnp.zeros((ncp, tq), F32)
    ps = []
    for h in range(hpg):
        s = jnp.where(valid_c, s_all[:, cols(h)] - slope(h) * dist_cf, NEG)
        e = jnp.exp2(s - jnp.max(s, axis=0, keepdims=True))
        p = e * (valid_cf * (1.0 / jnp.sum(e, axis=0, keepdims=True)))
        psum = psum + p
        ps.append(p.astype(BF16))
    ocmp_ref[...] = jnp.dot(vct_ref[...], jnp.concatenate(ps, axis=1), preferred_element_type=F32)

    cover_t = _cover_t(nselp, ncp)
    p1 = psum.astype(BF16)
    r1 = psum - p1.astype(F32)
    p2 = r1.astype(BF16)
    p3 = (r1 - p2.astype(F32)).astype(BF16)
    score = (jnp.dot(cover_t, p1, preferred_element_type=F32)
             + jnp.dot(cover_t, p2, preferred_element_type=F32)
             + jnp.dot(cover_t, p3, preferred_element_type=F32))
    j_col = _iota((nselp, 1), 0)
    cur = t_row // SEL_BLOCK
    forced = (j_col < N_INIT_BLOCKS) | ((cur - j_col >= 0) & (cur - j_col < N_LOCAL_BLOCKS))
    score = jnp.where(forced, FORCE, score)
    score = jnp.where(j_col * SEL_BLOCK <= t_row, score, NEG)
    score = jnp.where(j_col < n_sel, score, -jnp.inf)
    rank = jnp.zeros(score.shape, jnp.int32)
    for c in range(n_sel):
        sc = score[c:c + 1, :]
        rank = rank + ((sc > score) | ((sc == score) & (c < j_col))).astype(jnp.int32)
    selt_ref[...] = ((rank < n_pick) & (j_col < n_sel)).astype(F32)

    def init():
        m_ref[...] = jnp.full(m_ref.shape, NEG, F32)
        l_ref[...] = jnp.zeros(l_ref.shape, F32)
        acc_ref[...] = jnp.zeros(acc_ref.shape, F32)

    def flash_tile(k_tile, vt_tile, ok, shift):
        s_all = jnp.dot(k_tile, qt_ref[...], preferred_element_type=F32)
        ps, alphas = [], []
        for h in range(hpg):
            u = slope(h) * shift
            s = jnp.where(ok, s_all[:, cols(h)] - sd_ref[h], NEG)
            m_prev = m_ref[:, cols(h)]
            m_new = jnp.maximum(m_prev, jnp.max(s, axis=0, keepdims=True) - u)
            alpha = jnp.exp2(m_prev - m_new)
            p = jnp.exp2(s - (m_new + u))
            l_ref[:, cols(h)] = alpha * l_ref[:, cols(h)] + jnp.sum(p, axis=0, keepdims=True)
            m_ref[:, cols(h)] = m_new
            ps.append(p.astype(BF16))
            alphas.append(alpha)
        pv = jnp.dot(vt_tile, jnp.concatenate(ps, axis=1), preferred_element_type=F32)
        acc_ref[...] = jnp.concatenate(alphas, axis=1) * acc_ref[...] + pv

    init()

    def slc_step(kt, carry):
        k0 = pl.multiple_of(kt * tk, tk)
        picked = jnp.concatenate(
            [jnp.broadcast_to(selt_ref[pl.ds(kt * bpt + b, 1), :], (SEL_BLOCK, tq)) for b in range(bpt)],
            axis=0)
        ok = (picked > 0.5) & (d0 >= k0 - t0)
        flash_tile(sk_ref[pl.ds(k0, tk), :], vts_ref[:, pl.ds(k0, tk)], ok, (t0 - k0).astype(F32))
        return carry

    lax.fori_loop(0, (t0 + tq + tk - 1) // tk, slc_step, 0)
    oslc_ref[...] = acc_ref[...] * (1.0 / l_ref[...])

    init()

    def win_step(kt, carry):
        k0 = pl.multiple_of(kt * tk, tk)
        ok = (d0 >= k0 - t0) & (d0 < WINDOW + k0 - t0)
        flash_tile(wk_ref[pl.ds(k0, tk), :], vtw_ref[:, pl.ds(k0, tk)], ok, (t0 - k0).astype(F32))
        return carry

    lax.fori_loop(jnp.maximum(t0 - WINDOW + 1, 0) // tk, (t0 + tq + tk - 1) // tk, win_step, 0)
    inv_l = 1.0 / l_ref[...]

    gate_t = gate_ref[...].T
    for h in range(hpg):
        def grow(br):
            c = br * hpg + h
            return gate_t[c:c + 1, :]
        o_t = (grow(0) * ocmp_ref[:, cols(h)] + grow(1) * oslc_ref[:, cols(h)]
               + (grow(2) * inv_l[:, cols(h)]) * acc_ref[:, cols(h)])
        o_ref[:, h * HEAD_DIM:(h + 1) * HEAD_DIM] = o_t.T.astype(o_ref.dtype)


def nsa_prompt(q, gate, kc, vc, kvb, slopes, batch, seq, n_cmp, tq=256, tk=256):
    m, hd = q.shape
    hpg = hd // HEAD_DIM // KV_GROUPS
    gw = hpg * HEAD_DIM
    ncp = kc.shape[1]
    nq = seq // tq
    n_sel = seq // SEL_BLOCK
    nselp = -(-n_sel // SUBLANES) * SUBLANES
    assert seq % tq == 0 and seq % tk == 0 and tk % SEL_BLOCK == 0

    def kv_spec(proj):
        return pl.BlockSpec((seq, HEAD_DIM), lambda b, g, i: (b, proj * KV_GROUPS + g))

    return pl.pallas_call(
        functools.partial(_nsa_prompt_body, tq=tq, tk=tk, seq=seq, n_cmp=n_cmp, hpg=hpg),
        out_shape=_sds((m, hd), BF16),
        grid=(batch, KV_GROUPS, nq),
        in_specs=[pl.BlockSpec(memory_space=pltpu.SMEM),
                  pl.BlockSpec((tq, gw), lambda b, g, i: (b * nq + i, g)),
                  pl.BlockSpec((tq, LANES), lambda b, g, i: (b * nq + i, g)),
                  pl.BlockSpec((None, ncp, HEAD_DIM), lambda b, g, i: (b, 0, g)),
                  pl.BlockSpec((None, ncp, HEAD_DIM), lambda b, g, i: (b, 0, g)),
                  kv_spec(2), kv_spec(3), kv_spec(4), kv_spec(5)],
        out_specs=pl.BlockSpec((tq, gw), lambda b, g, i: (b * nq + i, g)),
        scratch_shapes=[pltpu.VMEM((HEAD_DIM, seq), BF16),
                        pltpu.VMEM((HEAD_DIM, seq), BF16),
                        pltpu.VMEM((HEAD_DIM, ncp), BF16),
                        pltpu.VMEM((hpg, tk, tq), F32),
                        pltpu.VMEM((HEAD_DIM, hpg * tq), BF16),
                        pltpu.VMEM((nselp, tq), F32),
                        pltpu.VMEM((1, hpg * tq), F32),
                        pltpu.VMEM((1, hpg * tq), F32),
                        pltpu.VMEM((HEAD_DIM, hpg * tq), F32),
                        pltpu.VMEM((HEAD_DIM, hpg * tq), F32),
                        pltpu.VMEM((HEAD_DIM, hpg * tq), F32)],
        compiler_params=_cparams(("parallel", "parallel", "arbitrary")),
        name="nsa_prompt",
    )(slopes, q, gate, kc, vc, kvb, kvb, kvb, kvb)


def _stack_heads(q, g, hpg):
    return jnp.concatenate(
        [q[:, (g * hpg + h) * HEAD_DIM:(g * hpg + h + 1) * HEAD_DIM] for h in range(hpg)],
        axis=0).astype(BF16)


def _nsa_sample_cmp_win_body(q_ref, kc_ref, vc_ref, cwk_ref, cwv_ref, nwk_ref, nwv_ref,
                             ocmp_ref, owin_ref, sel_ref, *, s, past, n_cmp, n_sel, hpg):
    q = q_ref[...].astype(F32)
    ncp = kc_ref.shape[0]
    nbp = sel_ref.shape[-1]
    wbuf = cwk_ref.shape[0]
    n_pick = min(N_SELECT, n_sel)
    pos = past + _iota((s, 1), 0)
    cover = _cover(ncp, nbp)
    n_row = _iota((1, ncp), 1)
    dist_c = pos - (n_row * CMP_STRIDE + (CMP_BLOCK - 1))
    valid_c = (dist_c >= 0) & (n_row < n_cmp)
    j_row = _iota((1, nbp), 1)
    cur = pos // SEL_BLOCK
    forced = (j_row < N_INIT_BLOCKS) | ((cur - j_row >= 0) & (cur - j_row < N_LOCAL_BLOCKS))
    pad_rows = LANES - s
    d_w1 = pos - (past - wbuf + _iota((1, wbuf), 1))
    ok_w1 = (d_w1 >= 0) & (d_w1 < WINDOW) & (past - wbuf + _iota((1, wbuf), 1) >= 0)
    d_w2 = pos - (past + _iota((1, LANES), 1))
    ok_w2 = (d_w2 >= 0) & (d_w2 < WINDOW) & (_iota((1, LANES), 1) < s)

    for g in range(KV_GROUPS):
        gs = slice(g * HEAD_DIM, (g + 1) * HEAD_DIM)
        qs = _stack_heads(q, g, hpg)
        slope3 = _head_slopes(g, hpg, (hpg, 1, 1), 0)

        s3 = _dot_nt(qs, kc_ref[:, gs]).reshape(hpg, s, ncp)
        s3 = jnp.where(valid_c[None], s3 - slope3 * dist_c.astype(F32)[None], NEG)
        e = jnp.exp2(s3 - jnp.max(s3, axis=-1, keepdims=True))
        p3 = e / jnp.sum(e, axis=-1, keepdims=True) * valid_c.astype(F32)[None]
        ocmp_ref[g] = jnp.dot(p3.reshape(hpg * s, ncp).astype(BF16), vc_ref[:, gs],
                              preferred_element_type=F32)

        score = _split3_dot(jnp.sum(p3, axis=0), cover)
        score = jnp.where(forced, FORCE, score)
        score = jnp.where(j_row * SEL_BLOCK <= pos, score, NEG)
        score = jnp.where(j_row < n_sel, score, -jnp.inf)
        sel = (_select_rank(score, j_row, n_sel) < n_pick) & (j_row < n_sel)
        sel_ref[g] = sel.astype(F32)

        zpad = jnp.zeros((pad_rows, HEAD_DIM), F32)
        k2 = jnp.concatenate([nwk_ref[:, gs], zpad], axis=0).astype(BF16)
        v2 = jnp.concatenate([nwv_ref[:, gs], zpad], axis=0).astype(BF16)
        s1 = _dot_nt(qs, cwk_ref[:, gs].astype(BF16)).reshape(hpg, s, wbuf)
        s2 = _dot_nt(qs, k2).reshape(hpg, s, LANES)
        s1 = jnp.where(ok_w1[None], s1 - slope3 * d_w1.astype(F32)[None], NEG)
        s2 = jnp.where(ok_w2[None], s2 - slope3 * d_w2.astype(F32)[None], NEG)
        mx = jnp.maximum(jnp.max(s1, axis=-1, keepdims=True), jnp.max(s2, axis=-1, keepdims=True))
        p1 = jnp.exp2(s1 - mx)
        p2 = jnp.exp2(s2 - mx)
        den = jnp.sum(p1, axis=-1, keepdims=True) + jnp.sum(p2, axis=-1, keepdims=True)
        ow = (jnp.dot(p1.reshape(hpg * s, wbuf).astype(BF16), cwv_ref[:, gs].astype(BF16),
                      preferred_element_type=F32)
              + jnp.dot(p2.reshape(hpg * s, LANES).astype(BF16), v2, preferred_element_type=F32))
        owin_ref[g] = ow / den.reshape(hpg * s, 1)


def nsa_sample_cmp_win(q, kc, vc, cwk, cwv, kv_new, past, n_cmp, n_sel):
    b, s, hd = q.shape
    hpg = hd // HEAD_DIM // KV_GROUPS
    ncp = kc.shape[1]
    wbuf = cwk.shape[1]
    gd = KV_GROUPS * HEAD_DIM
    nbp = -(-n_sel // LANES) * LANES
    o_shape = _sds((b, KV_GROUPS, hpg * s, HEAD_DIM), F32)
    o_spec = pl.BlockSpec((None, KV_GROUPS, hpg * s, HEAD_DIM), lambda i: (i, 0, 0, 0))
    return pl.pallas_call(
        functools.partial(_nsa_sample_cmp_win_body, s=s, past=past, n_cmp=n_cmp, n_sel=n_sel, hpg=hpg),
        out_shape=[o_shape, o_shape, _sds((b, KV_GROUPS, s, nbp), F32)],
        grid=(b,),
        in_specs=[pl.BlockSpec((None, s, hd), lambda i: (i, 0, 0)),
                  pl.BlockSpec((None, ncp, gd), lambda i: (i, 0, 0)),
                  pl.BlockSpec((None, ncp, gd), lambda i: (i, 0, 0)),
                  pl.BlockSpec((None, wbuf, gd), lambda i: (i, 0, 0)),
                  pl.BlockSpec((None, wbuf, gd), lambda i: (i, 0, 0)),
                  pl.BlockSpec((None, s, gd), lambda i: (i, 0, 4)),
                  pl.BlockSpec((None, s, gd), lambda i: (i, 0, 5))],
        out_specs=[o_spec, o_spec,
                   pl.BlockSpec((None, KV_GROUPS, s, nbp), lambda i: (i, 0, 0, 0))],
        compiler_params=_cparams(("parallel",)),
        name="nsa_sample_cmp_win",
    )(q, kc, vc, cwk, cwv, kv_new, kv_new)


def _nsa_sample_slc_body(*refs, s, past, hpg, pps, n_chunks):
    ids_ref = refs[0]
    del ids_ref
    kp_refs = refs[1:1 + pps]
    vp_refs = refs[1 + pps:1 + 2 * pps]
    (q_ref, sel_ref, nk_ref, nv_ref, gate_ref, ocmp_ref, owin_ref, o_ref,
     qs_ref, m_ref, l_ref, acc_ref) = refs[1 + 2 * pps:]
    c = pl.program_id(1)
    nbp = sel_ref.shape[-1]
    tk = pps * PAGE_SIZE
    pos = past + _iota((s, 1), 0)

    @pl.when(c == 0)
    def _():
        q = q_ref[...].astype(F32)
        for g in range(KV_GROUPS):
            qs_ref[g] = _stack_heads(q, g, hpg)
            _flash_init(m_ref.at[g], l_ref.at[g], acc_ref.at[g])

    k0 = c * tk
    d = pos - (k0 + _iota((1, tk), 1))
    blk = (k0 + _iota((nbp, tk), 1)) // SEL_BLOCK
    expand = (_iota((nbp, tk), 0) == blk).astype(BF16)
    for g in range(KV_GROUPS):
        gs = slice(g * HEAD_DIM, (g + 1) * HEAD_DIM)
        slope3 = _head_slopes(g, hpg, (hpg, 1, 1), 0)
        grp = pl.ds(g, PAGE_SIZE, stride=KV_GROUPS)
        kk = jnp.concatenate([r[grp, :] for r in kp_refs], axis=0).astype(BF16)
        vv = jnp.concatenate([r[grp, :] for r in vp_refs], axis=0).astype(BF16)
        picked = jnp.dot(sel_ref[g].astype(BF16), expand, preferred_element_type=F32)
        ok = (picked > 0.5) & (d >= 0)
        sc = _dot_nt(qs_ref[g], kk).reshape(hpg, s, tk)
        sc = jnp.where(ok[None], sc - slope3 * d.astype(F32)[None], NEG)
        _flash_step(sc, vv, m_ref.at[g], l_ref.at[g], acc_ref.at[g])

    @pl.when(c == n_chunks - 1)
    def _():
        jn = past // SEL_BLOCK
        d2 = pos - (past + _iota((1, LANES), 1))
        in_new = (d2 >= 0) & (_iota((1, LANES), 1) < s)
        zpad = jnp.zeros((LANES - s, HEAD_DIM), F32)
        gate = gate_ref[...]
        for g in range(KV_GROUPS):
            gs = slice(g * HEAD_DIM, (g + 1) * HEAD_DIM)
            slope3 = _head_slopes(g, hpg, (hpg, 1, 1), 0)
            k2 = jnp.concatenate([nk_ref[:, gs], zpad], axis=0).astype(BF16)
            v2 = jnp.concatenate([nv_ref[:, gs], zpad], axis=0).astype(BF16)
            ok = in_new & (sel_ref[g][:, jn:jn + 1] > 0.5)
            sc = _dot_nt(qs_ref[g], k2).reshape(hpg, s, LANES)
            sc = jnp.where(ok[None], sc - slope3 * d2.astype(F32)[None], NEG)
            _flash_step(sc, v2, m_ref.at[g], l_ref.at[g], acc_ref.at[g])
            o_slc = acc_ref[g] / l_ref[g]
            o_cmp = ocmp_ref[g].reshape(hpg, s, HEAD_DIM)
            o_win = owin_ref[g].reshape(hpg, s, HEAD_DIM)
            for h in range(hpg):
                def gcol(br):
                    col = g * LANES + br * hpg + h
                    return gate[:, col:col + 1]
                o = gcol(0) * o_cmp[h] + gcol(1) * o_slc[h] + gcol(2) * o_win[h]
                hs = (g * hpg + h) * HEAD_DIM
                o_ref[:, hs:hs + HEAD_DIM] = o


def nsa_sample_slc(q, sel, kv_new, gate, o_cmp, o_win, pool_k, pool_v, page_table, past,
                   pages_per_step=4):
    b, s, hd = q.shape
    hpg = hd // HEAD_DIM // KV_GROUPS
    gd = KV_GROUPS * HEAD_DIM
    npages = page_table.shape[1]
    pps = pages_per_step
    assert npages % pps == 0 and s <= SEL_BLOCK and past % SEL_BLOCK == 0
    n_chunks = npages // pps
    nbp = sel.shape[-1]

    def page_spec(p):
        return pl.BlockSpec((None, PAGE_SIZE * KV_GROUPS, HEAD_DIM),
                            lambda i, c, ids: (ids[i * npages + c * pps + p], 0, 0))

    def per_b(shape):
        nd = len(shape)
        return pl.BlockSpec((None,) + shape, lambda i, c, ids: (i,) + (0,) * nd)

    o4 = (KV_GROUPS, hpg * s, HEAD_DIM)
    grid_spec = pltpu.PrefetchScalarGridSpec(
        num_scalar_prefetch=1,
        grid=(b, n_chunks),
        in_specs=[page_spec(p) for p in range(pps)] + [page_spec(p) for p in range(pps)] + [
            per_b((s, hd)), per_b((KV_GROUPS, s, nbp)),
            pl.BlockSpec((None, s, gd), lambda i, c, ids: (i, 0, 2)),
            pl.BlockSpec((None, s, gd), lambda i, c, ids: (i, 0, 3)),
            per_b((s, KV_GROUPS * LANES)), per_b(o4), per_b(o4)],
        out_specs=per_b((s, hd)),
        scratch_shapes=[pltpu.VMEM((KV_GROUPS, hpg * s, HEAD_DIM), BF16),
                        pltpu.VMEM((KV_GROUPS, hpg, s, 1), F32),
                        pltpu.VMEM((KV_GROUPS, hpg, s, 1), F32),
                        pltpu.VMEM((KV_GROUPS, hpg, s, HEAD_DIM), F32)],
    )
    return pl.pallas_call(
        functools.partial(_nsa_sample_slc_body, s=s, past=past, hpg=hpg, pps=pps, n_chunks=n_chunks),
        out_shape=_sds((b, s, hd), F32),
        grid_spec=grid_spec,
        compiler_params=_cparams(("parallel", "arbitrary")),
        name="nsa_sample_slc",
    )(page_table.reshape(-1), *([pool_k] * pps), *([pool_v] * pps),
      q, sel, kv_new, kv_new, gate, o_cmp, o_win)


def _gate_weight(w_qg, hd, hpg):
    wg = w_qg[:, hd:].reshape(-1, N_BRANCH, KV_GROUPS, hpg).transpose(0, 2, 1, 3)
    wg = wg.reshape(-1, KV_GROUPS, N_BRANCH * hpg)
    wg = jnp.pad(wg, ((0, 0), (0, 0), (0, LANES - N_BRANCH * hpg)))
    return wg.reshape(-1, KV_GROUPS * LANES).astype(BF16)


def kernel(x_prompt, x_sample, state_conv_a, state_ffn_conv, cache_cmp_k, cache_cmp_v, cache_slc_k,
           cache_slc_v, cache_win_k, cache_win_v, page_table, g_attn, g_ffn, g_final, w_pw1, w_dw_a,
           ln_a_g, ln_a_b, w_pw2, g_kv, w_kv, cmp_pe_k, cmp_w1_k, cmp_w2_k, cmp_pe_v, cmp_w1_v,
           cmp_w2_v, w_qg, w_o, w_up, w_dw_f, w_down):
    bp, tp, d = x_prompt.shape
    bs, ts, _ = x_sample.shape
    depth = g_attn.shape[0]
    n_a = w_pw1.shape[0]
    assert depth == n_a + 1, "one NSA layer after the convolution layers"
    f = w_down.shape[1]
    hd = w_o.shape[1]
    hpg = hd // HEAD_DIM // KV_GROUPS
    gd = KV_GROUPS * HEAD_DIM
    n_pages = page_table.shape[1]
    past = n_pages * PAGE_SIZE
    scale = HEAD_DIM ** -0.5
    mp, ms = bp * tp, bs * ts
    kconv = w_dw_a.shape[1]
    kffn = w_dw_f.shape[1]

    xp = x_prompt.reshape(mp, d)
    xs = x_sample.reshape(ms, d)
    conv_p, conv_s, ffn_p, ffn_s = [], [], [], []

    w_down_b = w_down.astype(BF16)
    w_up_b = w_up.astype(BF16)

    def conv_ffn_both(xp, xs, layer):
        hp = rmsnorm(xp, g_ffn[layer], BF16)
        hs = rmsnorm(xs, g_ffn[layer], BF16)
        act, sta, stb = ffn_up_rows(hp, tp, w_up_b, layer, w_dw_f)
        ffn_p.append(jnp.concatenate([sta, stb], axis=-1)[:, STATE_ROWS - (kffn - 1):])
        xp = mm(act, w_down_b, layer=layer, bm=512, bn=512, resid=xp, vmem_mib=60, name="ffn_down")
        up_s = mm(hs, w_up_b, layer=layer, bm=ms, bn=512, name="ffn_up_sample").reshape(bs, ts, 2 * f)
        xx = jnp.concatenate([state_ffn_conv[layer], up_s], axis=1)
        ffn_s.append(xx[:, -(kffn - 1):])
        act_s = ffn_gate_sample(xx, ts, w_dw_f[layer]).reshape(ms, f).astype(BF16)
        xs = mm(act_s, w_down_b, layer=layer, bm=ms, bn=512, resid=xs, vmem_mib=60, name="ffn_down_sample")
        return xp, xs

    for a in range(n_a):
        hp = rmsnorm(xp, g_attn[a], BF16)
        hs = rmsnorm(xs, g_attn[a], BF16)
        up, us = mm2(hp, hs, w_pw1, layer=a, bn=256, glu=True, name="pw1_glu")
        conv_p.append(up.reshape(bp, tp, d)[:, -(kconv - 1):])
        cp = conv_ln_prompt(up, tp, w_dw_a[a], ln_a_g[a], ln_a_b[a])
        xx = jnp.concatenate([state_conv_a[a], us.reshape(bs, ts, d)], axis=1)
        conv_s.append(xx[:, -(kconv - 1):])
        cs = conv_ln_sample(xx, ts, w_dw_a[a], ln_a_g[a], ln_a_b[a]).reshape(ms, d).astype(BF16)
        xp, xs = mm2(cp, cs, w_pw2, layer=a, resid=(xp, xs), name="pw2")

        xp, xs = conv_ffn_both(xp, xs, a)

    wkv = w_kv.astype(BF16)
    kv_rows_p, kvb_p = kv_proj(rmsnorm(xp, g_kv, BF16), wkv)
    kv_p4 = [r.reshape(bp, tp, KV_GROUPS, HEAD_DIM) for r in kv_rows_p]
    kv_s = mm(rmsnorm(xs, g_kv, BF16), wkv, bm=ms, bn=512, name="kv_sample")
    kv_s5 = kv_s.reshape(bs, ts, 2 * N_BRANCH, KV_GROUPS, HEAD_DIM)
    kv_s3 = kv_s.reshape(bs, ts, 2 * N_BRANCH * gd)

    half_w = CMP_STRIDE * gd
    nhp = tp // CMP_STRIDE
    n_cmp_p = nhp - 1
    assert tp % CMP_STRIDE == 0 and ts <= CMP_STRIDE and past % PAGE_SIZE == 0
    nhs = past // CMP_STRIDE
    n_cmp_s = nhs
    page_ids = page_table.reshape(-1)
    kcs = []
    for proj, (pe, w1c, w2c, pool) in enumerate([(cmp_pe_k, cmp_w1_k, cmp_w2_k, cache_cmp_k),
                                                 (cmp_pe_v, cmp_w1_v, cmp_w2_v, cache_cmp_v)]):
        pebig, wbig = _cmp_weights(pe, w1c)
        w2b = w2c.astype(BF16)
        rows_per_page = PAGE_SIZE * KV_GROUPS
        lo, hi = cmp_proj_pool(kv_rows_p[proj].reshape(-1, rows_per_page, HEAD_DIM),
                               jnp.arange(mp // PAGE_SIZE, dtype=jnp.int32), pebig, wbig,
                               pages_per_step=tp // PAGE_SIZE)
        lo = lo.reshape(bp, nhp, gd)
        hi = hi.reshape(bp, nhp, gd)
        hi_next = jnp.concatenate([hi[:, 1:], jnp.zeros((bp, 1, gd), F32)], axis=1)
        _, c_p = cmp_out(lo, hi_next, w2b)

        lo_h, hi_h = cmp_proj_pool(pool.reshape(pool.shape[0], PAGE_SIZE * KV_GROUPS, HEAD_DIM),
                                   page_ids, pebig, wbig, pages_per_step=min(32, bs * n_pages))
        new_rows = jnp.pad(kv_s3[:, :, proj * gd:(proj + 1) * gd], ((0, 0), (0, CMP_STRIDE - ts), (0, 0)))
        new_halves = new_rows.reshape(bs, CMP_STRIDE, KV_GROUPS, HEAD_DIM).transpose(2, 0, 1, 3)
        _, hi_n = cmp_proj_halves(new_halves.reshape(KV_GROUPS, bs, CMP_STRIDE * HEAD_DIM), pebig, wbig)
        hi_next_s = jnp.concatenate([hi_h.reshape(bs, nhs, gd)[:, 1:], hi_n[:, None]], axis=1)
        _, c_s = cmp_out(lo_h.reshape(bs, nhs, gd), hi_next_s, w2b)
        kcs.append((c_p, c_s))
    (kc_p, kc_s), (vc_p, vc_s) = kcs

    b_l = n_a
    wg = _gate_weight(w_qg[0], hd, hpg)
    hp = rmsnorm(xp, g_attn[b_l], BF16)
    hs = rmsnorm(xs, g_attn[b_l], BF16)
    q_p, q_s = mm2(hp, hs, w_qg, layer=0, n=hd, scale=scale * LOG2E, out_dtype=BF16, name="q_proj")
    gate_p, gate_s = mm2(hp, hs, wg, act="sigmoid", name="gate_proj")
    n_heads = hd // HEAD_DIM
    slopes = jnp.exp2(-8.0 * jnp.arange(1, n_heads + 1, dtype=F32) / n_heads) * LOG2E
    o_p = nsa_prompt(q_p, gate_p, kc_p, vc_p, kvb_p, slopes, bp, tp, n_cmp_p)

    q_s3 = q_s.reshape(bs, ts, hd)
    n_sel_s = past // SEL_BLOCK + -(-ts // SEL_BLOCK)
    wbuf = cache_win_k.shape[1]
    o_cmp_s, o_win_s, sel_s = nsa_sample_cmp_win(
        q_s3, kc_s, vc_s, cache_win_k.reshape(bs, wbuf, gd), cache_win_v.reshape(bs, wbuf, gd),
        kv_s3, past, n_cmp_s, n_sel_s)
    o_s = nsa_sample_slc(q_s3, sel_s, kv_s3, gate_s.reshape(bs, ts, KV_GROUPS * LANES), o_cmp_s, o_win_s,
                         cache_slc_k.reshape(-1, PAGE_SIZE * KV_GROUPS, HEAD_DIM),
                         cache_slc_v.reshape(-1, PAGE_SIZE * KV_GROUPS, HEAD_DIM),
                         page_table, past, pages_per_step=min(32, n_pages))
    xp, xs = mm2(o_p, o_s.reshape(ms, hd).astype(BF16), w_o, layer=0, resid=(xp, xs), name="o_proj")

    xp, xs = conv_ffn_both(xp, xs, b_l)

    y_prompt = rmsnorm(xp, g_final, F32).reshape(bp, tp, d)
    y_sample = rmsnorm(xs, g_final, F32).reshape(bs, ts, d)
    keep = min(WINDOW, tp)
    win_k_s = jnp.concatenate([cache_win_k, kv_s5[:, :, 4]], axis=1)[:, -wbuf:]
    win_v_s = jnp.concatenate([cache_win_v, kv_s5[:, :, 5]], axis=1)[:, -wbuf:]
    return (y_prompt, y_sample, jnp.stack(conv_p), jnp.stack(conv_s), jnp.stack(ffn_p), jnp.stack(ffn_s),
            kv_p4[0], kv_p4[1], kv_p4[2], kv_p4[3],
            kv_s5[:, :, 0], kv_s5[:, :, 1], kv_s5[:, :, 2], kv_s5[:, :, 3],
            kv_p4[4][:, -keep:], kv_p4[5][:, -keep:], win_k_s, win_v_s)
```
